```python
import math
import jax, jax.numpy as jnp
from jax import lax
import numpy as np

D_MODEL = 1024
BATCH = 4
SEQ = 4096
DEPTH = 4
DEC_BATCH = 128
DEC_SEQ = 4
PAST_LEN = 2048
PAGE_SIZE = 128

D_MIX = 2 * D_MODEL
D_SSD = D_MIX // 2
D_ATTN = D_MIX - D_SSD
SSD_HEAD_DIM = 64
SSD_HEADS = D_SSD // SSD_HEAD_DIM
SSD_GROUPS = 2
SSD_STATE = 128
SSD_CONV = 4
SSD_CHUNK = 128
CONV_DIM = D_SSD + 2 * SSD_GROUPS * SSD_STATE
ATTN_HEAD_DIM = 64
ATTN_V_DIM = 2 * ATTN_HEAD_DIM
ATTN_HEADS = D_ATTN // ATTN_V_DIM
ATTN_KV_HEADS = 4
ATTN_REP = ATTN_HEADS // ATTN_KV_HEADS
Q_DIM = ATTN_HEADS * 2 * ATTN_HEAD_DIM
K_DIM = ATTN_KV_HEADS * 2 * ATTN_HEAD_DIM
V_DIM = ATTN_KV_HEADS * ATTN_V_DIM
IN_DIM = D_SSD + CONV_DIM + SSD_HEADS + Q_DIM + K_DIM + V_DIM
SPLIT_POINTS = (D_SSD, D_SSD + CONV_DIM, D_SSD + CONV_DIM + SSD_HEADS,
                D_SSD + CONV_DIM + SSD_HEADS + Q_DIM,
                D_SSD + CONV_DIM + SSD_HEADS + Q_DIM + K_DIM)
D_FF = 4 * D_MODEL
Q_BLOCK = 128
EPS = 1e-6

kernel_name = 'hymba_ssd_diffattn_decoder_step'


def rmsnorm(x, g):
    x32 = x.astype(jnp.float32)
    y = x32 * lax.rsqrt(jnp.mean(x32 * x32, axis=-1, keepdims=True) + EPS)
    return y.astype(x.dtype) * g


def alibi_slopes():
    s = np.array([2.0 ** (-8.0 * (i + 1) / ATTN_HEADS) for i in range(ATTN_HEADS)], np.float32)
    return jnp.asarray(s.reshape(ATTN_KV_HEADS, ATTN_REP))


def causal_conv(u, buf, w, bias):
    T = u.shape[1]
    up = jnp.concatenate([buf.astype(u.dtype), u], axis=1)
    out = bias
    for i in range(SSD_CONV):
        out = out + up[:, i:i + T] * w[i]
    return jax.nn.silu(out), up[:, T:]


def ssd_scan(x, dt, A, B, C, h0, chunk):
    f32 = jnp.float32
    b, T, H, P = x.shape
    G, N = B.shape[2], B.shape[3]
    K = H // G
    nc = T // chunk
    xdt = (x.astype(f32) * dt[..., None]).reshape(b, nc, chunk, G, K, P)
    a = (dt * A).reshape(b, nc, chunk, G, K)
    Bc = B.astype(f32).reshape(b, nc, chunk, G, N)
    Cc = C.astype(f32).reshape(b, nc, chunk, G, N)
    a_cs = jnp.cumsum(a, axis=2)
    causal = jnp.tril(jnp.ones((chunk, chunk), bool))
    seg = a_cs[:, :, :, None] - a_cs[:, :, None, :]
    decay = jnp.exp(jnp.where(causal[:, :, None, None], seg, -jnp.inf))
    cb = jnp.einsum('bclgn,bcsgn->bclsg', Cc, Bc)
    y_intra = jnp.einsum('bclsgk,bcsgkp->bclgkp', cb[..., None] * decay, xdt)
    to_end = jnp.exp(a_cs[:, :, -1:] - a_cs)
    chunk_states = jnp.einsum('bclgn,bclgk,bclgkp->bcgkpn', Bc, to_end, xdt)
    chunk_decay = jnp.exp(a_cs[:, :, -1])

    def carry_state(h, inp):
        s_c, d_c = inp
        return h * d_c[..., None, None] + s_c, h

    h_last, h_start = lax.scan(carry_state, h0.astype(f32).reshape(b, G, K, P, N),
                               (jnp.moveaxis(chunk_states, 1, 0), jnp.moveaxis(chunk_decay, 1, 0)))
    h_start = jnp.moveaxis(h_start, 0, 1)
    y_inter = jnp.einsum('bclgn,bclgk,bcgkpn->bclgkp', Cc, jnp.exp(a_cs), h_start)
    return (y_intra + y_inter).reshape(b, T, H, P), h_last.reshape(b, H, P, N)


def ssd_branch(z, xbc, dt_raw, conv_buf, h0, conv_w, conv_b, dt_bias, a_log, d_skip, norm_g):
    b, T, _ = z.shape
    xbc, new_conv = causal_conv(xbc, conv_buf, conv_w, conv_b)
    xs, Bm, Cm = jnp.split(xbc, (D_SSD, D_SSD + SSD_GROUPS * SSD_STATE), axis=-1)
    xs = xs.reshape(b, T, SSD_HEADS, SSD_HEAD_DIM)
    Bm = Bm.reshape(b, T, SSD_GROUPS, SSD_STATE)
    Cm = Cm.reshape(b, T, SSD_GROUPS, SSD_STATE)
    dt = jax.nn.softplus(dt_raw.astype(jnp.float32) + dt_bias.astype(jnp.float32))
    A = -jnp.exp(a_log.astype(jnp.float32))
    chunk = SSD_CHUNK if T % SSD_CHUNK == 0 else T
    y, h_new = ssd_scan(xs, dt, A, Bm, Cm, h0, chunk)
    y = y + xs.astype(jnp.float32) * d_skip.astype(jnp.float32)[:, None]
    y = y.reshape(b, T, D_SSD) * jax.nn.silu(z.astype(jnp.float32))
    return rmsnorm(y, norm_g).astype(z.dtype), new_conv, h_new.astype(h0.dtype)


def diff_attend(q, q_pos, k, v, k_pos, lam, slopes):
    b, tq = q.shape[0], q.shape[1]
    s = jnp.einsum('bqkrch,bskch->bkrcqs', q, k).astype(jnp.float32) * (ATTN_HEAD_DIM ** -0.5)
    dist = (q_pos[:, None] - k_pos[None, :]).astype(jnp.float32)
    s = s - slopes[:, :, None, None, None] * dist
    s = jnp.where(dist >= 0, s, -jnp.inf)
    p = jax.nn.softmax(s, axis=-1)
    a = p[:, :, :, 0] - lam * p[:, :, :, 1]
    o = jnp.einsum('bkrqs,bskv->bqkrv', a.astype(v.dtype), v)
    return o.reshape(b, tq, ATTN_HEADS, ATTN_V_DIM)


def prompt_diff_attention(q, k, v, lam, slopes):
    b, T = q.shape[0], q.shape[1]
    nb = T // Q_BLOCK
    q_blocks = jnp.moveaxis(q.reshape((b, nb, Q_BLOCK) + q.shape[2:]), 1, 0)
    q_pos = jnp.arange(T, dtype=jnp.int32).reshape(nb, Q_BLOCK)
    k_pos = jnp.arange(T, dtype=jnp.int32)
    out = lax.map(lambda blk: diff_attend(blk[0], blk[1], k, v, k_pos, lam, slopes), (q_blocks, q_pos))
    return jnp.moveaxis(out, 0, 1).reshape(b, T, ATTN_HEADS, ATTN_V_DIM)


def mixer(h, w_in_l, conv_w_l, conv_b_l, dt_bias_l, a_log_l, d_skip_l, ssd_norm_l, subln_l, w_out_l,
          lam, lam_init, slopes, conv_buf, h0, k_past, v_past):
    b, T, _ = h.shape
    z, xbc, dt_raw, q, k, v = jnp.split(h @ w_in_l, SPLIT_POINTS, axis=-1)
    y_ssd, new_conv, new_h = ssd_branch(z, xbc, dt_raw, conv_buf, h0, conv_w_l, conv_b_l,
                                        dt_bias_l, a_log_l, d_skip_l, ssd_norm_l)
    q = q.reshape(b, T, ATTN_KV_HEADS, ATTN_REP, 2, ATTN_HEAD_DIM)
    k_rows = k.reshape(b, T, ATTN_KV_HEADS, 2 * ATTN_HEAD_DIM)
    v_rows = v.reshape(b, T, ATTN_KV_HEADS, ATTN_V_DIM)
    if k_past is None:
        o = prompt_diff_attention(q, k_rows.reshape(b, T, ATTN_KV_HEADS, 2, ATTN_HEAD_DIM),
                                  v_rows, lam, slopes)
    else:
        past = k_past.shape[1]
        k_all = jnp.concatenate([k_past.astype(k_rows.dtype), k_rows], axis=1)
        k_all = k_all.reshape(b, past + T, ATTN_KV_HEADS, 2, ATTN_HEAD_DIM)
        v_all = jnp.concatenate([v_past.astype(v_rows.dtype), v_rows], axis=1)
        q_pos = past + jnp.arange(T, dtype=jnp.int32)
        k_pos = jnp.arange(past + T, dtype=jnp.int32)
        o = diff_attend(q, q_pos, k_all, v_all, k_pos, lam, slopes)
    o = rmsnorm(o, subln_l) * (1.0 - lam_init)
    mixed = jnp.concatenate([y_ssd, o.reshape(b, T, D_ATTN)], axis=-1) @ w_out_l
    return mixed, k_rows, v_rows, new_h, new_conv


def mlp(h, w_up_l, w_down_l):
    return jnp.square(jax.nn.relu(h @ w_up_l)) @ w_down_l


def setup_inputs(seed: int = 0) -> dict:
    key = jax.random.key(seed)
    ks = jax.random.split(key, 24)
    nrm = jax.random.normal
    n_pages = PAST_LEN // PAGE_SIZE
    n_used = DEC_BATCH * n_pages
    n_pool = n_used + max(1, n_used // 4)
    x_prompt = nrm(ks[0], (BATCH, SEQ, D_MODEL), jnp.float32)
    x_sample = nrm(ks[1], (DEC_BATCH, DEC_SEQ, D_MODEL), jnp.float32)
    cache_k = nrm(ks[2], (DEPTH, n_pool, PAGE_SIZE, ATTN_KV_HEADS, 2 * ATTN_HEAD_DIM), jnp.float32)
    cache_v = nrm(ks[3], (DEPTH, n_pool, PAGE_SIZE, ATTN_KV_HEADS, ATTN_V_DIM), jnp.float32)
    state_ssm = 0.1 * nrm(ks[4], (DEPTH, DEC_BATCH, SSD_HEADS, SSD_HEAD_DIM, SSD_STATE), jnp.float32)
    state_conv = nrm(ks[5], (DEPTH, DEC_BATCH, SSD_CONV - 1, CONV_DIM), jnp.float32)
    page_table = jax.random.permutation(ks[6], n_pool)[:n_used].reshape(DEC_BATCH, n_pages).astype(jnp.int32)
    norm_mix = 1.0 + 0.02 * nrm(ks[7], (DEPTH, D_MODEL), jnp.float32)
    w_in = nrm(ks[8], (DEPTH, D_MODEL, IN_DIM), jnp.float32) * D_MODEL ** -0.5
    conv_w = nrm(ks[9], (DEPTH, SSD_CONV, CONV_DIM), jnp.float32) * SSD_CONV ** -0.5
    conv_b = 0.02 * nrm(ks[10], (DEPTH, CONV_DIM), jnp.float32)
    dt0 = jnp.exp(jax.random.uniform(ks[11], (DEPTH, SSD_HEADS), jnp.float32,
                                     math.log(1e-3), math.log(1e-1)))
    dt_bias = dt0 + jnp.log(-jnp.expm1(-dt0))
    A_log = jnp.log(jax.random.uniform(ks[12], (DEPTH, SSD_HEADS), jnp.float32, 1.0, 16.0))
    D_skip = 1.0 + 0.02 * nrm(ks[13], (DEPTH, SSD_HEADS), jnp.float32)
    ssd_norm = 1.0 + 0.02 * nrm(ks[14], (DEPTH, D_SSD), jnp.float32)
    lambda_q1 = 0.1 * nrm(ks[15], (DEPTH, ATTN_HEAD_DIM), jnp.float32)
    lambda_k1 = 0.1 * nrm(ks[16], (DEPTH, ATTN_HEAD_DIM), jnp.float32)
    lambda_q2 = 0.1 * nrm(ks[17], (DEPTH, ATTN_HEAD_DIM), jnp.float32)
    lambda_k2 = 0.1 * nrm(ks[18], (DEPTH, ATTN_HEAD_DIM), jnp.float32)
    attn_subln = 1.0 + 0.02 * nrm(ks[19], (DEPTH, ATTN_V_DIM), jnp.float32)
    w_out = nrm(ks[20], (DEPTH, D_MIX, D_MODEL), jnp.float32) * D_MIX ** -0.5
    norm_mlp = 1.0 + 0.02 * nrm(ks[21], (DEPTH, D_MODEL), jnp.float32)
    w_up = nrm(ks[22], (DEPTH, D_MODEL, D_FF), jnp.float32) * D_MODEL ** -0.5
    k2 = jax.random.split(ks[23], 2)
    w_down = nrm(k2[0], (DEPTH, D_FF, D_MODEL), jnp.float32) * D_FF ** -0.5
    norm_final = 1.0 + 0.02 * nrm(k2[1], (D_MODEL,), jnp.float32)
    return {'x_prompt': x_prompt, 'x_sample': x_sample, 'cache_k': cache_k, 'cache_v': cache_v,
            'state_ssm': state_ssm, 'state_conv': state_conv, 'page_table': page_table,
            'norm_mix': norm_mix, 'w_in': w_in, 'conv_w': conv_w, 'conv_b': conv_b,
            'dt_bias': dt_bias, 'A_log': A_log, 'D_skip': D_skip, 'ssd_norm': ssd_norm,
            'lambda_q1': lambda_q1, 'lambda_k1': lambda_k1, 'lambda_q2': lambda_q2,
            'lambda_k2': lambda_k2, 'attn_subln': attn_subln, 'w_out': w_out,
            'norm_mlp': norm_mlp, 'w_up': w_up, 'w_down': w_down, 'norm_final': norm_final}


def reference(x_prompt, x_sample, cache_k, cache_v, state_ssm, state_conv, page_table,
              norm_mix, w_in, conv_w, conv_b, dt_bias, A_log, D_skip, ssd_norm,
              lambda_q1, lambda_k1, lambda_q2, lambda_k2, attn_subln, w_out,
              norm_mlp, w_up, w_down, norm_final):
    f32 = jnp.float32
    slopes = alibi_slopes()
    past_len = page_table.shape[1] * cache_k.shape[2]
    xp, xs = x_prompt, x_sample
    bp, bs = xp.shape[0], xs.shape[0]
    kp_l, vp_l, hp_l, cp_l = [], [], [], []
    ks_l, vs_l, hs_l, cs_l = [], [], [], []
    for l in range(DEPTH):
        lam_init = 0.8 - 0.6 * math.exp(-0.3 * l)
        lam = (jnp.exp(jnp.sum(lambda_q1[l].astype(f32) * lambda_k1[l].astype(f32)))
               - jnp.exp(jnp.sum(lambda_q2[l].astype(f32) * lambda_k2[l].astype(f32))) + lam_init)
        layer_w = (w_in[l], conv_w[l], conv_b[l], dt_bias[l], A_log[l], D_skip[l],
                   ssd_norm[l], attn_subln[l], w_out[l])
        mix_p, k_p, v_p, h_p, c_p = mixer(
            rmsnorm(xp, norm_mix[l]), *layer_w, lam, lam_init, slopes,
            jnp.zeros((bp, SSD_CONV - 1, CONV_DIM), xp.dtype),
            jnp.zeros((bp, SSD_HEADS, SSD_HEAD_DIM, SSD_STATE), state_ssm.dtype), None, None)
        xp = xp + mix_p
        xp = xp + mlp(rmsnorm(xp, norm_mlp[l]), w_up[l], w_down[l])
        k_past = cache_k[l, page_table].reshape(bs, past_len, ATTN_KV_HEADS, 2 * ATTN_HEAD_DIM)
        v_past = cache_v[l, page_table].reshape(bs, past_len, ATTN_KV_HEADS, ATTN_V_DIM)
        mix_s, k_s, v_s, h_s, c_s = mixer(
            rmsnorm(xs, norm_mix[l]), *layer_w, lam, lam_init, slopes,
            state_conv[l], state_ssm[l], k_past, v_past)
        xs = xs + mix_s
        xs = xs + mlp(rmsnorm(xs, norm_mlp[l]), w_up[l], w_down[l])
        kp_l.append(k_p); vp_l.append(v_p); hp_l.append(h_p); cp_l.append(c_p)
        ks_l.append(k_s); vs_l.append(v_s); hs_l.append(h_s); cs_l.append(c_s)
    y_prompt = rmsnorm(xp, norm_final)
    y_sample = rmsnorm(xs, norm_final)
    return (y_prompt, y_sample,
            jnp.stack(kp_l), jnp.stack(vp_l), jnp.stack(hp_l), jnp.stack(cp_l),
            jnp.stack(ks_l), jnp.stack(vs_l), jnp.stack(hs_l), jnp.stack(cs_l))
```

```python
import functools
import math

import numpy as np
import jax
import jax.numpy as jnp
from jax import lax
from jax.experimental import pallas as pl
from jax.experimental.pallas import tpu as pltpu

f32 = jnp.float32
bf16 = jnp.bfloat16

D_MODEL = 1024
D_SSD = 1024
D_ATTN = 1024
SSD_HEAD_DIM = 64
SSD_HEADS = 16
SSD_GROUPS = 2
SSD_STATE = 128
SSD_CONV = 4
SSD_CHUNK = 128
CONV_DIM = D_SSD + 2 * SSD_GROUPS * SSD_STATE
ATTN_HEAD_DIM = 64
ATTN_V_DIM = 128
ATTN_HEADS = 8
ATTN_KV_HEADS = 4
ATTN_REP = 2
Q_DIM = ATTN_HEADS * 2 * ATTN_HEAD_DIM
K_DIM = ATTN_KV_HEADS * 2 * ATTN_HEAD_DIM
V_DIM = ATTN_KV_HEADS * ATTN_V_DIM
D_FF = 4 * D_MODEL
EPS = 1e-6
NEG = -1e30

LANES = 128
SUBLANES = 8
VMEM_LIMIT_BYTES = 56 * 1024 * 1024

SAMPLE_ROWS = SUBLANES
GROUP_W = D_SSD // SSD_GROUPS
HEADS_PER_GROUP = SSD_HEADS // SSD_GROUPS
N_PRE = SSD_CONV - 1


def _row_tile(m, want):
    t = min(want, m)
    while m % t:
        t -= SUBLANES
    return t


def _cparams(*sem):
    return pltpu.CompilerParams(dimension_semantics=sem, vmem_limit_bytes=VMEM_LIMIT_BYTES)


def _const_spec(shape):
    nd = len(shape)
    return pl.BlockSpec(shape, lambda *_: (0,) * nd, pipeline_mode=pl.Buffered(1))


def _rms(x, g):
    return x * lax.rsqrt(jnp.mean(x * x, axis=-1, keepdims=True) + EPS) * g


def _silu(x):
    return x * jax.nn.sigmoid(x)


def _dot(a, b):
    return jnp.dot(a, b, preferred_element_type=f32)


def _dot_nt(a, b):
    return lax.dot_general(a, b, (((1,), (1,)), ((), ())), preferred_element_type=f32)


def _inproj_body(x_ref, g_ref, wz, wxbc, wdt, wq, wk, wv,
                 z_o, xbc_o, dt_o, q_o, k_o, v_o, kb_o, vb_o):
    xb = _rms(x_ref[...], g_ref[...]).astype(bf16)
    z_o[...] = _dot(xb, wz[...])
    xbc_o[...] = _dot(xb, wxbc[...])
    dt_o[...] = _dot(xb, wdt[...])
    q_o[...] = (_dot(xb, wq[...]) * (ATTN_HEAD_DIM ** -0.5)).astype(bf16)
    k = _dot(xb, wk[...])
    k_o[...] = k
    kb_o[...] = k.astype(bf16)
    v = _dot(xb, wv[...])
    v_o[...] = v
    vb_o[...] = v.astype(bf16)


def _in_proj(x, g, w, tm):
    m = x.shape[0]
    tm = _row_tile(m, tm)
    widths = (D_SSD, CONV_DIM, LANES, Q_DIM, K_DIM, V_DIM)
    row = lambda n: pl.BlockSpec((tm, n), lambda i: (i, 0))
    out_dt = (f32, f32, f32, bf16, f32, f32, bf16, bf16)
    out_w = (D_SSD, CONV_DIM, LANES, Q_DIM, K_DIM, V_DIM, K_DIM, V_DIM)
    return pl.pallas_call(
        _inproj_body,
        grid=(m // tm,),
        in_specs=[row(D_MODEL), _const_spec((1, D_MODEL))] + [_const_spec((D_MODEL, n)) for n in widths],
        out_specs=[row(n) for n in out_w],
        out_shape=[jax.ShapeDtypeStruct((m, n), d) for n, d in zip(out_w, out_dt)],
        compiler_params=_cparams("parallel"),
        name="in_proj",
    )(x, g, *w)


def _outmlp_body(y_ref, o_ref, x_ref, woy, woo, g_ref, wup, wdn, gf_ref, out_ref, *, final, ff_chunk):
    x1 = x_ref[...] + _dot(y_ref[...], woy[...]) + _dot(o_ref[...], woo[...])
    h = _rms(x1, g_ref[...]).astype(bf16)
    mlp = None
    for j in range(D_FF // ff_chunk):
        u = _dot(h, wup[:, j * ff_chunk:(j + 1) * ff_chunk])
        u = jnp.square(jnp.maximum(u, 0.0)).astype(bf16)
        d = _dot(u, wdn[j * ff_chunk:(j + 1) * ff_chunk, :])
        mlp = d if mlp is None else mlp + d
    x2 = x1 + mlp
    out_ref[...] = _rms(x2, gf_ref[...]) if final else x2


def _out_mlp(y, o, x, woy, woo, g, wup, wdn, gf, tm, final):
    m = x.shape[0]
    tm = _row_tile(m, tm)
    row = lambda n: pl.BlockSpec((tm, n), lambda i: (i, 0))
    return pl.pallas_call(
        functools.partial(_outmlp_body, final=final, ff_chunk=1024),
        grid=(m // tm,),
        in_specs=[row(D_SSD), row(D_ATTN), row(D_MODEL),
                  _const_spec((D_SSD, D_MODEL)), _const_spec((D_ATTN, D_MODEL)), _const_spec((1, D_MODEL)),
                  _const_spec((D_MODEL, D_FF)), _const_spec((D_FF, D_MODEL)), _const_spec((1, D_MODEL))],
        out_specs=row(D_MODEL),
        out_shape=jax.ShapeDtypeStruct((m, D_MODEL), f32),
        compiler_params=_cparams("parallel"),
        name="out_mlp",
    )(y, o, x, woy, woo, g, wup, wdn, gf)


def _split3(x):
    p1 = x.astype(bf16)
    r1 = x - p1.astype(f32)
    p2 = r1.astype(bf16)
    p3 = (r1 - p2.astype(f32)).astype(bf16)
    return p1, p2, p3


def _expand(v, emat):
    p1, p2, p3 = _split3(v)
    return _dot(p1, emat) + _dot(p2, emat) + _dot(p3, emat)


def _eye_bf16():
    ri = lax.broadcasted_iota(jnp.int32, (LANES, LANES), 0)
    ci = lax.broadcasted_iota(jnp.int32, (LANES, LANES), 1)
    return jnp.where(ri == ci, 1.0, 0.0).astype(bf16)


def _transpose_f32(x, eye):
    p1, p2, p3 = _split3(x)
    return _dot_nt(eye, p1) + _dot_nt(eye, p2) + _dot_nt(eye, p3)


def _conv_taps(cbuf, w, bias, L):
    conv = bias
    for i in range(SSD_CONV):
        off = SUBLANES - N_PRE + i
        conv = conv + w[i:i + 1, :] * cbuf[off:off + L, :]
    return conv


def _intra_chunk(cm, bm, xdt, a_cs, a_cs_t, mask):
    L = cm.shape[0]
    lane = lax.broadcasted_iota(jnp.int32, (L, LANES), 1)
    halves = (lane < SSD_HEAD_DIM, lane >= SSD_HEAD_DIM)
    out = []
    for g in range(SSD_GROUPS):
        cg = cm[:, g * SSD_STATE:(g + 1) * SSD_STATE].astype(bf16)
        bg = bm[:, g * SSD_STATE:(g + 1) * SSD_STATE].astype(bf16)
        cbm = _dot_nt(cg, bg)
        blocks = []
        for jp in range(HEADS_PER_GROUP // 2):
            pair = g * (HEADS_PER_GROUP // 2) + jp
            xp = xdt[:, pair * LANES:(pair + 1) * LANES]
            acc = None
            for k in range(2):
                h = 2 * pair + k
                seg = a_cs[:, h:h + 1] - a_cs_t[h:h + 1, :]
                dec = jnp.where(mask, jnp.exp(seg), 0.0)
                mh = (cbm * dec).astype(bf16)
                part = _dot(mh, jnp.where(halves[k], xp, 0.0).astype(bf16))
                acc = part if acc is None else acc + part
            blocks.append(acc)
        out.append(blocks)
    return out


def _ssd_prompt_body(z_ref, xbc_ref, dt_ref, cw_ref, cb_ref, dtb_ref, alog_ref, dsk_ref, g_ref, e_ref,
                     y_ref, hout_ref, cbuf, hT, *, L, n_chunks):
    c = pl.program_id(1)

    @pl.when(c == 0)
    def _():
        cbuf[0:SUBLANES, :] = jnp.zeros((SUBLANES, CONV_DIM), f32)
        hT[...] = jnp.zeros_like(hT)

    xraw = xbc_ref[...]
    cbuf[SUBLANES:SUBLANES + L, :] = xraw
    act = _silu(_conv_taps(cbuf, cw_ref[...], cb_ref[...], L))
    cbuf[0:SUBLANES, :] = xraw[L - SUBLANES:L, :]
    xs = act[:, :D_SSD]
    bm = act[:, D_SSD:D_SSD + SSD_GROUPS * SSD_STATE]
    cm = act[:, D_SSD + SSD_GROUPS * SSD_STATE:]

    rows = lax.broadcasted_iota(jnp.int32, (L, LANES), 0)
    dt = jax.nn.softplus(dt_ref[...] + dtb_ref[...])
    a = dt * (-jnp.exp(alog_ref[...]))
    a_cs = a
    s = 1
    while s < L:
        a_cs = a_cs + jnp.where(rows >= s, pltpu.roll(a_cs, s, 0), 0.0)
        s *= 2
    ea = jnp.exp(a_cs)
    te = jnp.exp(a_cs[L - 1:L, :] - a_cs)

    emat = e_ref[...]
    ea_e = _expand(ea, emat)
    xdt = xs * _expand(dt, emat)
    wst = (xdt * _expand(te, emat)).astype(bf16)

    eye = _eye_bf16()
    a_cs_t = _transpose_f32(a_cs, eye)
    causal = lax.broadcasted_iota(jnp.int32, (L, L), 0) >= lax.broadcasted_iota(jnp.int32, (L, L), 1)
    intra = _intra_chunk(cm, bm, xdt, a_cs, a_cs_t, causal)

    y_parts = []
    for g in range(SSD_GROUPS):
        gs = slice(g * GROUP_W, (g + 1) * GROUP_W)
        cg = cm[:, g * SSD_STATE:(g + 1) * SSD_STATE].astype(bf16)
        bg = bm[:, g * SSD_STATE:(g + 1) * SSD_STATE].astype(bf16)
        h_g = hT[:, gs]
        y_inter = _dot(cg, h_g.astype(bf16)) * ea_e[:, gs]
        bg_t = _dot_nt(eye, bg).astype(bf16)
        hT[:, gs] = h_g * ea_e[L - 1:L, gs] + _dot(bg_t, wst[:, gs])
        y_parts.append(jnp.concatenate(intra[g], axis=1) + y_inter)
    y = jnp.concatenate(y_parts, axis=1) + xs * dsk_ref[...]
    y = y * _silu(z_ref[...])
    y_ref[...] = _rms(y, g_ref[...]).astype(y_ref.dtype)

    @pl.when(c == n_chunks - 1)
    def _():
        hout_ref[...] = hT[...].T


def _ssd_weight_specs():
    return [_const_spec((SSD_CONV, CONV_DIM)), _const_spec((1, CONV_DIM)), _const_spec((1, LANES)),
            _const_spec((1, LANES)), _const_spec((1, D_SSD)), _const_spec((1, D_SSD)),
            _const_spec((LANES, D_SSD))]


def _ssd_prompt(z, xbc, dt, ssd_w, *, batch, L, n_chunks):
    hp = SSD_HEADS * SSD_HEAD_DIM
    rowspec = lambda n: pl.BlockSpec((L, n), lambda b, c: (b * n_chunks + c, 0))
    return pl.pallas_call(
        functools.partial(_ssd_prompt_body, L=L, n_chunks=n_chunks),
        grid=(batch, n_chunks),
        in_specs=[rowspec(D_SSD), rowspec(CONV_DIM), rowspec(LANES)] + _ssd_weight_specs(),
        out_specs=[rowspec(D_SSD), pl.BlockSpec((None, hp, SSD_STATE), lambda b, c: (b, 0, 0))],
        out_shape=[jax.ShapeDtypeStruct((batch * n_chunks * L, D_SSD), bf16),
                   jax.ShapeDtypeStruct((batch, hp, SSD_STATE), f32)],
        scratch_shapes=[pltpu.VMEM((SUBLANES + L, CONV_DIM), f32), pltpu.VMEM((SSD_STATE, hp), f32)],
        compiler_params=_cparams("parallel", "arbitrary"),
        name="ssd_prompt",
    )(z, xbc, dt, *ssd_w)


def _ssd_sample_body(z_ref, xbc_ref, dt_ref, pre_ref, h0_ref, cw_ref, cb_ref, dtb_ref, alog_ref, dsk_ref, g_ref,
                     e_ref, y_ref, hout_ref, cbuf, yint, *, n_seq, ts):
    R = SAMPLE_ROWS
    L = n_seq * R
    first = R - ts

    cbuf[0:SUBLANES, :] = jnp.zeros((SUBLANES, CONV_DIM), f32)
    cbuf[SUBLANES:SUBLANES + L, :] = xbc_ref[...]
    for s in range(n_seq):
        r0 = SUBLANES + s * R + first - N_PRE
        cbuf[r0:r0 + N_PRE, :] = pre_ref[s]
    act = _silu(_conv_taps(cbuf, cw_ref[...], cb_ref[...], L))
    xs = act[:, :D_SSD]
    bm = act[:, D_SSD:D_SSD + SSD_GROUPS * SSD_STATE]
    cm = act[:, D_SSD + SSD_GROUPS * SSD_STATE:]

    slot = jnp.bitwise_and(lax.broadcasted_iota(jnp.int32, (L, LANES), 0), R - 1)
    dt = jnp.where(slot >= first, jax.nn.softplus(dt_ref[...] + dtb_ref[...]), 0.0)
    a = dt * (-jnp.exp(alog_ref[...]))
    a_cs = a
    a_sf = a
    s = 1
    while s < R:
        a_cs = a_cs + jnp.where(slot >= s, pltpu.roll(a_cs, s, 0), 0.0)
        a_sf = a_sf + jnp.where(slot < R - s, pltpu.roll(a_sf, L - s, 0), 0.0)
        s *= 2
    ea = jnp.exp(a_cs)
    te = jnp.exp(a_sf - a)
    cd = jnp.exp(a_cs + a_sf - a)

    emat = e_ref[...]
    ea_e = _expand(ea, emat)
    xdt = xs * _expand(dt, emat)
    w_t = (xdt * _expand(te, emat)).T
    cd_t = _expand(cd, emat).T

    eye = _eye_bf16()
    a_cs_t = _transpose_f32(a_cs, eye)
    li = lax.broadcasted_iota(jnp.int32, (L, L), 0)
    si = lax.broadcasted_iota(jnp.int32, (L, L), 1)
    mask = jnp.logical_and(li >= si, jnp.bitwise_and(li, -R) == jnp.bitwise_and(si, -R))
    intra = _intra_chunk(cm, bm, xdt, a_cs, a_cs_t, mask)

    col = lax.broadcasted_iota(jnp.int32, (GROUP_W, L), 1)
    cmb = cm.astype(bf16)
    bmb = bm.astype(bf16)
    for s in range(n_seq):
        rs = slice(s * R, (s + 1) * R)
        mine = jnp.logical_and(col >= s * R, col < (s + 1) * R)
        for g in range(SSD_GROUPS):
            gs = slice(g * GROUP_W, (g + 1) * GROUP_W)
            ns = slice(g * SSD_STATE, (g + 1) * SSD_STATE)
            h0 = h0_ref[s, gs, :]
            yint[rs, gs] = _dot_nt(cmb[rs, ns], h0.astype(bf16))
            upd = _dot(jnp.where(mine, w_t[gs, :], 0.0).astype(bf16), bmb[:, ns])
            hout_ref[s, gs, :] = h0 * cd_t[gs, s * R:s * R + 1] + upd

    y = jnp.concatenate(intra[0] + intra[1], axis=1) + yint[...] * ea_e + xs * dsk_ref[...]
    y = y * _silu(z_ref[...])
    y_ref[...] = _rms(y, g_ref[...]).astype(y_ref.dtype)


def _ssd_sample(z, xbc, dt, pre, h0, ssd_w, *, batch, ts, n_seq):
    hp = SSD_HEADS * SSD_HEAD_DIM
    L = n_seq * SAMPLE_ROWS
    assert batch % n_seq == 0
    rowspec = lambda n: pl.BlockSpec((L, n), lambda i: (i, 0))
    seqspec = lambda a, b: pl.BlockSpec((n_seq, a, b), lambda i: (i, 0, 0))
    return pl.pallas_call(
        functools.partial(_ssd_sample_body, n_seq=n_seq, ts=ts),
        grid=(batch // n_seq,),
        in_specs=[rowspec(D_SSD), rowspec(CONV_DIM), rowspec(LANES), seqspec(N_PRE, CONV_DIM),
                  seqspec(hp, SSD_STATE)] + _ssd_weight_specs(),
        out_specs=[rowspec(D_SSD), seqspec(hp, SSD_STATE)],
        out_shape=[jax.ShapeDtypeStruct((batch * SAMPLE_ROWS, D_SSD), bf16),
                   jax.ShapeDtypeStruct((batch, hp, SSD_STATE), f32)],
        scratch_shapes=[pltpu.VMEM((SUBLANES + L, CONV_DIM), f32), pltpu.VMEM((L, D_SSD), f32)],
        compiler_params=_cparams("parallel"),
        name="ssd_sample",
    )(z, xbc, dt, pre, h0, *ssd_w)


def _lambda(lamp_ref, lam_init):
    lp = lamp_ref[...]
    t1 = jnp.sum(lp[0:1, :] * lp[1:2, :], axis=-1, keepdims=True)
    t2 = jnp.sum(lp[2:3, :] * lp[3:4, :], axis=-1, keepdims=True)
    return jnp.exp(t1) - jnp.exp(t2) + lam_init


def _mask_maps(qr, lane):
    zero = jnp.zeros_like(qr)
    return jnp.where(lane < ATTN_HEAD_DIM, qr, zero), jnp.where(lane >= ATTN_HEAD_DIM, qr, zero)


def _pattn_body(slopes_ref, q_ref, k_ref, v_ref, lamp_ref, g_ref, o_ref,
                q4, s_ref, p_ref, m_ref, l_ref, acc_ref, *, tq, lam_init):
    kv = pl.program_id(1)
    qi = pl.program_id(2)
    n_grp = 2 * ATTN_REP

    lane = lax.broadcasted_iota(jnp.int32, (tq, LANES), 1)
    for r in range(ATTN_REP):
        m1, m2 = _mask_maps(q_ref[:, r * LANES:(r + 1) * LANES], lane)
        q4[r * tq:(r + 1) * tq, :] = m1
        q4[(ATTN_REP + r) * tq:(ATTN_REP + r + 1) * tq, :] = m2
    m_ref[...] = jnp.full_like(m_ref, NEG)
    l_ref[...] = jnp.zeros_like(l_ref)
    acc_ref[...] = jnp.zeros_like(acc_ref)

    rel = (lax.broadcasted_iota(jnp.int32, (tq, tq), 0) - lax.broadcasted_iota(jnp.int32, (tq, tq), 1)).astype(f32)
    slope = [slopes_ref[kv * ATTN_REP + r] for r in range(ATTN_REP)]

    def step(j, masked):
        start = pl.multiple_of(j * tq, tq)
        kblk = k_ref[pl.ds(start, tq), :]
        vblk = v_ref[pl.ds(start, tq), :]
        s_ref[...] = _dot_nt(q4[...], kblk)
        dist = rel + ((qi - j) * tq).astype(f32)
        for g in range(n_grp):
            rs = slice(g * tq, (g + 1) * tq)
            sg = s_ref[rs, :] - slope[g % ATTN_REP] * dist
            if masked:
                sg = jnp.where(rel >= 0.0, sg, NEG)
            m_old = m_ref[rs, :]
            m_new = jnp.maximum(m_old, jnp.max(sg, axis=-1, keepdims=True))
            alpha = jnp.exp(m_old - m_new)
            p = jnp.exp(sg - m_new)
            l_ref[rs, :] = alpha * l_ref[rs, :] + jnp.sum(p, axis=-1, keepdims=True)
            m_ref[rs, :] = m_new
            acc_ref[rs, :] = alpha * acc_ref[rs, :]
            p_ref[rs, :] = p.astype(bf16)
        acc_ref[...] += _dot(p_ref[...], vblk)

    def loop_body(j, carry):
        step(j, False)
        return carry

    lax.fori_loop(0, qi, loop_body, 0)
    step(qi, True)

    lam = _lambda(lamp_ref, lam_init)
    for r in range(ATTN_REP):
        r1 = slice(r * tq, (r + 1) * tq)
        r2 = slice((ATTN_REP + r) * tq, (ATTN_REP + r + 1) * tq)
        o = acc_ref[r1, :] / l_ref[r1, :] - lam * (acc_ref[r2, :] / l_ref[r2, :])
        o = _rms(o, g_ref[...]) * (1.0 - lam_init)
        o_ref[:, r * LANES:(r + 1) * LANES] = o.astype(o_ref.dtype)


def _prompt_attention(slopes, q, kb, vb, lamp, g, *, batch, T, tq, lam_init):
    nq = T // tq
    rows = 2 * ATTN_REP * tq
    return pl.pallas_call(
        functools.partial(_pattn_body, tq=tq, lam_init=lam_init),
        grid=(batch, ATTN_KV_HEADS, nq),
        in_specs=[pl.BlockSpec(memory_space=pltpu.SMEM),
                  pl.BlockSpec((tq, ATTN_REP * LANES), lambda b, h, i: (b * nq + i, h)),
                  pl.BlockSpec((T, LANES), lambda b, h, i: (b, h)),
                  pl.BlockSpec((T, LANES), lambda b, h, i: (b, h)),
                  _const_spec((4, ATTN_HEAD_DIM)), _const_spec((1, ATTN_V_DIM))],
        out_specs=pl.BlockSpec((tq, ATTN_REP * LANES), lambda b, h, i: (b * nq + i, h)),
        out_shape=jax.ShapeDtypeStruct((batch * T, D_ATTN), bf16),
        scratch_shapes=[pltpu.VMEM((rows, LANES), bf16), pltpu.VMEM((rows, tq), f32), pltpu.VMEM((rows, tq), bf16),
                        pltpu.VMEM((rows, 1), f32), pltpu.VMEM((rows, 1), f32), pltpu.VMEM((rows, LANES), f32)],
        compiler_params=_cparams("parallel", "parallel", "arbitrary"),
        name="prompt_attention",
    )(slopes, q, kb, vb, lamp, g)


def _sattn_body(pt_ref, slopes_ref, q_ref, kn_ref, vn_ref, *rest, n_pages, page, ts, lam_init):
    k_pages = rest[:n_pages]
    v_pages = rest[n_pages:2 * n_pages]
    lamp_ref, g_ref, o_ref = rest[2 * n_pages:]
    del pt_ref
    past = n_pages * page
    R = SAMPLE_ROWS
    first = R - ts
    width = past + page

    row = lax.broadcasted_iota(jnp.int32, (R, width), 0)
    col = lax.broadcasted_iota(jnp.int32, (R, width), 1)
    is_new = col >= past
    kpos = col - jnp.where(is_new, first, 0)
    dist = (past + row - first - kpos).astype(f32)
    ok = jnp.logical_and(dist >= 0.0, jnp.logical_or(jnp.logical_not(is_new), col >= past + first))
    lane = lax.broadcasted_iota(jnp.int32, (R, LANES), 1)
    lam = _lambda(lamp_ref, lam_init)
    q = q_ref[...].astype(f32)
    zpad = jnp.zeros((page - R, LANES), bf16)

    for h in range(ATTN_KV_HEADS):
        groups = []
        for r in range(ATTN_REP):
            c0 = (h * ATTN_REP + r) * LANES
            groups.append(_mask_maps(q[:, c0:c0 + LANES], lane))
        q32 = jnp.concatenate([groups[0][0], groups[1][0], groups[0][1], groups[1][1]], axis=0).astype(bf16)
        hs = slice(h * LANES, (h + 1) * LANES)
        kcat = jnp.concatenate([kp[:, h, :].astype(bf16) for kp in k_pages]
                               + [kn_ref[:, hs].astype(bf16), zpad], axis=0)
        vcat = jnp.concatenate([vp[:, h, :].astype(bf16) for vp in v_pages]
                               + [vn_ref[:, hs].astype(bf16), zpad], axis=0)
        s = _dot_nt(q32, kcat)
        ps = []
        ls = []
        for g in range(2 * ATTN_REP):
            slope = slopes_ref[h * ATTN_REP + g % ATTN_REP]
            sg = jnp.where(ok, s[g * R:(g + 1) * R, :] - slope * dist, NEG)
            p = jnp.exp(sg - jnp.max(sg, axis=-1, keepdims=True))
            ls.append(jnp.sum(p, axis=-1, keepdims=True))
            ps.append(p.astype(bf16))
        acc = _dot(jnp.concatenate(ps, axis=0), vcat)
        for r in range(ATTN_REP):
            g1, g2 = r, ATTN_REP + r
            o = acc[g1 * R:(g1 + 1) * R, :] / ls[g1] - lam * (acc[g2 * R:(g2 + 1) * R, :] / ls[g2])
            o = _rms(o, g_ref[...]) * (1.0 - lam_init)
            c0 = (h * ATTN_REP + r) * LANES
            o_ref[:, c0:c0 + LANES] = o.astype(o_ref.dtype)


def _sample_attention(page_table, slopes, q, kn, vn, cache_k, cache_v, lamp, g, *, layer, ts, lam_init):
    bs, n_pages = page_table.shape
    page = cache_k.shape[2]
    R = SAMPLE_ROWS

    def page_spec(p):
        return pl.BlockSpec((None, None, page, ATTN_KV_HEADS, LANES),
                            lambda b, pt: (layer, pt[b, p], 0, 0, 0))

    seq = lambda n: pl.BlockSpec((R, n), lambda b, pt: (b, 0))
    const = lambda shape: pl.BlockSpec(shape, lambda b, pt: (0,) * len(shape), pipeline_mode=pl.Buffered(1))
    grid_spec = pltpu.PrefetchScalarGridSpec(
        num_scalar_prefetch=1,
        grid=(bs,),
        in_specs=[pl.BlockSpec(memory_space=pltpu.SMEM), seq(Q_DIM), seq(K_DIM), seq(V_DIM)]
        + [page_spec(p) for p in range(n_pages)] * 2
        + [const((4, ATTN_HEAD_DIM)), const((1, ATTN_V_DIM))],
        out_specs=seq(D_ATTN),
    )
    return pl.pallas_call(
        functools.partial(_sattn_body, n_pages=n_pages, page=page, ts=ts, lam_init=lam_init),
        grid_spec=grid_spec,
        out_shape=jax.ShapeDtypeStruct((bs * R, D_ATTN), bf16),
        compiler_params=_cparams("parallel"),
        name="sample_attention",
    )(page_table, slopes, q, kn, vn, *([cache_k] * n_pages), *([cache_v] * n_pages), lamp, g)


def _alibi_slopes():
    return jnp.asarray(np.array([2.0 ** (-8.0 * (i + 1) / ATTN_HEADS) for i in range(ATTN_HEADS)], np.float32))


def _expand_matrix():
    e = np.zeros((LANES, D_SSD), np.float32)
    for h in range(SSD_HEADS):
        e[h, h * SSD_HEAD_DIM:(h + 1) * SSD_HEAD_DIM] = 1.0
    return jnp.asarray(e, bf16)


def _pad_lanes(v):
    return jnp.pad(v, (0, LANES - v.shape[0]))[None, :]


def kernel(x_prompt, x_sample, cache_k, cache_v, state_ssm, state_conv, page_table, norm_mix, w_in, conv_w, conv_b, dt_bias, A_log, D_skip, ssd_norm, lambda_q1, lambda_k1, lambda_q2, lambda_k2, attn_subln, w_out, norm_mlp, w_up, w_down, norm_final):
    depth = w_in.shape[0]
    bp, T, _ = x_prompt.shape
    bs, ts, _ = x_sample.shape
    R = SAMPLE_ROWS
    assert T % SSD_CHUNK == 0 and N_PRE <= ts <= R - N_PRE and ts % SSD_CHUNK != 0
    mp = bp * T
    hp = SSD_HEADS * SSD_HEAD_DIM

    slopes = _alibi_slopes()
    emat = _expand_matrix()
    gf = norm_final[None, :]

    xp = x_prompt.reshape(mp, D_MODEL)
    xs = jnp.pad(x_sample, ((0, 0), (R - ts, 0), (0, 0))).reshape(bs * R, D_MODEL)
    state4 = state_ssm.reshape(depth, bs, hp, SSD_STATE)

    o0 = D_SSD
    o1 = o0 + CONV_DIM
    o2 = o1 + SSD_HEADS
    o3 = o2 + Q_DIM
    o4 = o3 + K_DIM

    outs = {k: [] for k in ("kp", "vp", "hp", "cp", "ks", "vs", "hs", "cs")}
    for l in range(depth):
        lam_init = 0.8 - 0.6 * math.exp(-0.3 * l)
        wl = w_in[l]
        w_parts = (wl[:, :o0].astype(bf16), wl[:, o0:o1].astype(bf16),
                   jnp.pad(wl[:, o1:o2], ((0, 0), (0, LANES - SSD_HEADS))).astype(bf16),
                   wl[:, o2:o3].astype(bf16), wl[:, o3:o4].astype(bf16), wl[:, o4:].astype(bf16))
        g_mix = norm_mix[l][None, :]
        ssd_w = (conv_w[l], conv_b[l][None, :], _pad_lanes(dt_bias[l]), _pad_lanes(A_log[l]),
                 jnp.repeat(D_skip[l], SSD_HEAD_DIM)[None, :], ssd_norm[l][None, :], emat)
        lamp = jnp.stack([lambda_q1[l], lambda_k1[l], lambda_q2[l], lambda_k2[l]])
        g_sub = attn_subln[l][None, :]
        woy = w_out[l, :D_SSD].astype(bf16)
        woo = w_out[l, D_SSD:].astype(bf16)
        mlp_w = (norm_mlp[l][None, :], w_up[l].astype(bf16), w_down[l].astype(bf16), gf)
        final = l == depth - 1

        z, xbc, dt, q, k, v, kb, vb = _in_proj(xp, g_mix, w_parts, 512)
        y, h_new = _ssd_prompt(z, xbc, dt, ssd_w, batch=bp, L=SSD_CHUNK, n_chunks=T // SSD_CHUNK)
        o = _prompt_attention(slopes, q, kb, vb, lamp, g_sub, batch=bp, T=T, tq=256, lam_init=lam_init)
        xp = _out_mlp(y, o, xp, woy, woo, *mlp_w, 512, final)
        outs["kp"].append(k.reshape(bp, T, ATTN_KV_HEADS, 2 * ATTN_HEAD_DIM))
        outs["vp"].append(v.reshape(bp, T, ATTN_KV_HEADS, ATTN_V_DIM))
        outs["hp"].append(h_new.reshape(bp, SSD_HEADS, SSD_HEAD_DIM, SSD_STATE))
        outs["cp"].append(xbc.reshape(bp, T, CONV_DIM)[:, T - N_PRE:, :])

        z, xbc, dt, q, k, v, kb, vb = _in_proj(xs, g_mix, w_parts, 512)
        y, h_new = _ssd_sample(z, xbc, dt, state_conv[l], state4[l], ssd_w, batch=bs, ts=ts, n_seq=16)
        o = _sample_attention(page_table, slopes, q, k, v, cache_k, cache_v, lamp, g_sub,
                              layer=l, ts=ts, lam_init=lam_init)
        xs = _out_mlp(y, o, xs, woy, woo, *mlp_w, 512, final)
        outs["ks"].append(k.reshape(bs, R, ATTN_KV_HEADS, 2 * ATTN_HEAD_DIM)[:, R - ts:])
        outs["vs"].append(v.reshape(bs, R, ATTN_KV_HEADS, ATTN_V_DIM)[:, R - ts:])
        outs["hs"].append(h_new.reshape(bs, SSD_HEADS, SSD_HEAD_DIM, SSD_STATE))
        outs["cs"].append(xbc.reshape(bs, R, CONV_DIM)[:, R - N_PRE:])

    y_prompt = xp.reshape(bp, T, D_MODEL)
    y_sample = xs.reshape(bs, R, D_MODEL)[:, R - ts:]
    st = lambda key: jnp.stack(outs[key])
    return (y_prompt, y_sample, st("kp"), st("vp"), st("hp"), st("cp"), st("ks"), st("vs"), st("hs"), st("cs"))
```

```python
import functools
import math

import numpy as np
import jax
import jax.numpy as jnp
from jax import lax
from jax.experimental import pallas as pl
from jax.experimental.pallas import tpu as pltpu

f32 = jnp.float32
bf16 = jnp.bfloat16

D_MODEL = 1024
D_SSD = 1024
D_ATTN = 1024
SSD_HEAD_DIM = 64
SSD_HEADS = 16
SSD_GROUPS = 2
SSD_STATE = 128
SSD_CONV = 4
SSD_CHUNK = 128
CONV_DIM = D_SSD + 2 * SSD_GROUPS * SSD_STATE
ATTN_HEAD_DIM = 64
ATTN_V_DIM = 128
ATTN_HEADS = 8
ATTN_KV_HEADS = 4
ATTN_REP = 2
Q_DIM = ATTN_HEADS * 2 * ATTN_HEAD_DIM
K_DIM = ATTN_KV_HEADS * 2 * ATTN_HEAD_DIM
V_DIM = ATTN_KV_HEADS * ATTN_V_DIM
D_FF = 4 * D_MODEL
EPS = 1e-6
NEG = -1e30

LANES = 128
SUBLANES = 8
VMEM_LIMIT_BYTES = 56 * 1024 * 1024

SAMPLE_ROWS = SUBLANES
GROUP_W = D_SSD // SSD_GROUPS
HEADS_PER_GROUP = SSD_HEADS // SSD_GROUPS
N_PRE = SSD_CONV - 1


def _row_tile(m, want):
    t = min(want, m)
    while m % t:
        t -= SUBLANES
    return t


def _cparams(*sem):
    return pltpu.CompilerParams(dimension_semantics=sem, vmem_limit_bytes=VMEM_LIMIT_BYTES)


def _const_spec(shape):
    nd = len(shape)
    return pl.BlockSpec(shape, lambda *_: (0,) * nd, pipeline_mode=pl.Buffered(1))


def _rms(x, g):
    return x * lax.rsqrt(jnp.mean(x * x, axis=-1, keepdims=True) + EPS) * g


def _silu(x):
    return x * jax.nn.sigmoid(x)


def _dot(a, b):
    return jnp.dot(a, b, preferred_element_type=f32)


def _dot_nt(a, b):
    return lax.dot_general(a, b, (((1,), (1,)), ((), ())), preferred_element_type=f32)


IN_WIDTHS = (D_SSD, CONV_DIM, LANES, Q_DIM, K_DIM, V_DIM)


def _inproj_common(x_ref, g_ref, wz, wxbc, wdt, wq, z_o, xbc_o, dt_o):
    xb = _rms(x_ref[...], g_ref[...]).astype(bf16)
    z_o[...] = _dot(xb, wz[...])
    xbc_o[...] = _dot(xb, wxbc[...])
    dt_o[...] = _dot(xb, wdt[...])
    return xb, _dot(xb, wq[...]) * (ATTN_HEAD_DIM ** -0.5)


def _inproj_sample_body(x_ref, g_ref, wz, wxbc, wdt, wq, wk, wv, z_o, xbc_o, dt_o, q_o, k_o, v_o):
    xb, q = _inproj_common(x_ref, g_ref, wz, wxbc, wdt, wq, z_o, xbc_o, dt_o)
    q_o[...] = q.astype(bf16)
    k_o[...] = _dot(xb, wk[...])
    v_o[...] = _dot(xb, wv[...])


def _inproj_prompt_body(x_ref, g_ref, wz, wxbc, wdt, wq, wk, wv, *rest, tk, n_prev):
    z_o, xbc_o, dt_o, qT_o, kb_o, vT_o, kst_o, vst_o = rest[n_prev:]
    xb, q = _inproj_common(x_ref, g_ref, wz, wxbc, wdt, wq, z_o, xbc_o, dt_o)
    tm = q.shape[0]
    qT_o[...] = q.T.astype(bf16)
    k = _dot(xb, wk[...])
    v = _dot(xb, wv[...])
    kb_o[...] = k.astype(bf16)
    vT = v.T.astype(bf16)
    for i in range(tm // tk):
        vT_o[i] = vT[:, i * tk:(i + 1) * tk]
    for h in range(ATTN_KV_HEADS):
        hs = slice(h * LANES, (h + 1) * LANES)
        kst_o[pl.ds(h, tm, stride=ATTN_KV_HEADS), :] = k[:, hs]
        vst_o[pl.ds(h, tm, stride=ATTN_KV_HEADS), :] = v[:, hs]


def _in_proj_sample(x, g, w, tm):
    m = x.shape[0]
    tm = _row_tile(m, tm)
    row = lambda n: pl.BlockSpec((tm, n), lambda i: (i, 0))
    out_dt = (f32, f32, f32, bf16, f32, f32)
    return pl.pallas_call(
        _inproj_sample_body,
        grid=(m // tm,),
        in_specs=[row(D_MODEL), _const_spec((1, D_MODEL))] + [_const_spec((D_MODEL, n)) for n in IN_WIDTHS],
        out_specs=[row(n) for n in IN_WIDTHS],
        out_shape=[jax.ShapeDtypeStruct((m, n), d) for n, d in zip(IN_WIDTHS, out_dt)],
        compiler_params=_cparams("parallel"),
        name="in_proj_sample",
    )(x, g, *w)


def _in_proj_prompt(x, g, w, prev_stacks, *, layer, depth, tm, tk):
    m = x.shape[0]
    tm = _row_tile(m, tm)
    assert tm % tk == 0
    row = lambda n: pl.BlockSpec((tm, n), lambda i: (i, 0))
    n_prev = len(prev_stacks)
    stack_spec = pl.BlockSpec((None, tm * ATTN_KV_HEADS, LANES), lambda i: (layer, i, 0))
    stack_shape = jax.ShapeDtypeStruct((depth, m * ATTN_KV_HEADS, LANES), f32)
    n_in = 2 + len(IN_WIDTHS)
    return pl.pallas_call(
        functools.partial(_inproj_prompt_body, tk=tk, n_prev=n_prev),
        grid=(m // tm,),
        in_specs=[row(D_MODEL), _const_spec((1, D_MODEL))] + [_const_spec((D_MODEL, n)) for n in IN_WIDTHS]
        + [pl.BlockSpec(memory_space=pl.ANY)] * n_prev,
        out_specs=[row(D_SSD), row(CONV_DIM), row(LANES),
                   pl.BlockSpec((Q_DIM, tm), lambda i: (0, i)), row(K_DIM),
                   pl.BlockSpec((tm // tk, V_DIM, tk), lambda i: (i, 0, 0)), stack_spec, stack_spec],
        out_shape=[jax.ShapeDtypeStruct((m, D_SSD), f32), jax.ShapeDtypeStruct((m, CONV_DIM), f32),
                   jax.ShapeDtypeStruct((m, LANES), f32), jax.ShapeDtypeStruct((Q_DIM, m), bf16),
                   jax.ShapeDtypeStruct((m, K_DIM), bf16), jax.ShapeDtypeStruct((m // tk, V_DIM, tk), bf16),
                   stack_shape, stack_shape],
        input_output_aliases={n_in + j: 6 + j for j in range(n_prev)},
        compiler_params=_cparams("parallel"),
        name="in_proj_prompt",
    )(x, g, *w, *prev_stacks)


def _outmlp_body(y_ref, o_ref, x_ref, woy, woo, g_ref, wup, wdn, gf_ref, out_ref, *, final, ff_chunk):
    x1 = x_ref[...] + _dot(y_ref[...], woy[...]) + _dot(o_ref[...], woo[...])
    h = _rms(x1, g_ref[...]).astype(bf16)
    mlp = None
    for j in range(D_FF // ff_chunk):
        u = _dot(h, wup[:, j * ff_chunk:(j + 1) * ff_chunk])
        u = jnp.square(jnp.maximum(u, 0.0)).astype(bf16)
        d = _dot(u, wdn[j * ff_chunk:(j + 1) * ff_chunk, :])
        mlp = d if mlp is None else mlp + d
    x2 = x1 + mlp
    out_ref[...] = _rms(x2, gf_ref[...]) if final else x2


def _out_mlp(y, o, x, woy, woo, g, wup, wdn, gf, tm, final):
    m = x.shape[0]
    tm = _row_tile(m, tm)
    row = lambda n: pl.BlockSpec((tm, n), lambda i: (i, 0))
    return pl.pallas_call(
        functools.partial(_outmlp_body, final=final, ff_chunk=1024),
        grid=(m // tm,),
        in_specs=[row(D_SSD), row(D_ATTN), row(D_MODEL),
                  _const_spec((D_SSD, D_MODEL)), _const_spec((D_ATTN, D_MODEL)), _const_spec((1, D_MODEL)),
                  _const_spec((D_MODEL, D_FF)), _const_spec((D_FF, D_MODEL)), _const_spec((1, D_MODEL))],
        out_specs=row(D_MODEL),
        out_shape=jax.ShapeDtypeStruct((m, D_MODEL), f32),
        compiler_params=_cparams("parallel"),
        name="out_mlp",
    )(y, o, x, woy, woo, g, wup, wdn, gf)


def _split3(x):
    p1 = x.astype(bf16)
    r1 = x - p1.astype(f32)
    p2 = r1.astype(bf16)
    p3 = (r1 - p2.astype(f32)).astype(bf16)
    return p1, p2, p3


def _expand(v, emat):
    p1, p2, p3 = _split3(v)
    return _dot(p1, emat) + _dot(p2, emat) + _dot(p3, emat)


def _eye_bf16():
    ri = lax.broadcasted_iota(jnp.int32, (LANES, LANES), 0)
    ci = lax.broadcasted_iota(jnp.int32, (LANES, LANES), 1)
    return jnp.where(ri == ci, 1.0, 0.0).astype(bf16)


def _transpose_f32(x, eye):
    p1, p2, p3 = _split3(x)
    return _dot_nt(eye, p1) + _dot_nt(eye, p2) + _dot_nt(eye, p3)


def _conv_taps(cbuf, w, bias, L):
    conv = bias
    for i in range(SSD_CONV):
        off = SUBLANES - N_PRE + i
        conv = conv + w[i:i + 1, :] * cbuf[off:off + L, :]
    return conv


def _intra_chunk(cm, bm, xdt, a_cs, a_cs_t, mask):
    L = cm.shape[0]
    lane = lax.broadcasted_iota(jnp.int32, (L, LANES), 1)
    halves = (lane < SSD_HEAD_DIM, lane >= SSD_HEAD_DIM)
    out = []
    for g in range(SSD_GROUPS):
        cg = cm[:, g * SSD_STATE:(g + 1) * SSD_STATE].astype(bf16)
        bg = bm[:, g * SSD_STATE:(g + 1) * SSD_STATE].astype(bf16)
        cbm = _dot_nt(cg, bg)
        blocks = []
        for jp in range(HEADS_PER_GROUP // 2):
            pair = g * (HEADS_PER_GROUP // 2) + jp
            xp = xdt[:, pair * LANES:(pair + 1) * LANES]
            acc = None
            for k in range(2):
                h = 2 * pair + k
                seg = a_cs[:, h:h + 1] - a_cs_t[h:h + 1, :]
                dec = jnp.where(mask, jnp.exp(seg), 0.0)
                mh = (cbm * dec).astype(bf16)
                part = _dot(mh, jnp.where(halves[k], xp, 0.0).astype(bf16))
                acc = part if acc is None else acc + part
            blocks.append(acc)
        out.append(blocks)
    return out


def _ssd_prompt_body(z_ref, xbc_ref, dt_ref, cw_ref, cb_ref, dtb_ref, alog_ref, dsk_ref, g_ref, e_ref,
                     y_ref, hout_ref, cbuf, hT, *, L, n_chunks):
    c = pl.program_id(1)

    @pl.when(c == 0)
    def _():
        cbuf[0:SUBLANES, :] = jnp.zeros((SUBLANES, CONV_DIM), f32)
        hT[...] = jnp.zeros_like(hT)

    xraw = xbc_ref[...]
    cbuf[SUBLANES:SUBLANES + L, :] = xraw
    act = _silu(_conv_taps(cbuf, cw_ref[...], cb_ref[...], L))
    cbuf[0:SUBLANES, :] = xraw[L - SUBLANES:L, :]
    xs = act[:, :D_SSD]
    bm = act[:, D_SSD:D_SSD + SSD_GROUPS * SSD_STATE]
    cm = act[:, D_SSD + SSD_GROUPS * SSD_STATE:]

    rows = lax.broadcasted_iota(jnp.int32, (L, LANES), 0)
    dt = jax.nn.softplus(dt_ref[...] + dtb_ref[...])
    a = dt * (-jnp.exp(alog_ref[...]))
    a_cs = a
    s = 1
    while s < L:
        a_cs = a_cs + jnp.where(rows >= s, pltpu.roll(a_cs, s, 0), 0.0)
        s *= 2
    ea = jnp.exp(a_cs)
    te = jnp.exp(a_cs[L - 1:L, :] - a_cs)

    emat = e_ref[...]
    ea_e = _expand(ea, emat)
    xdt = xs * _expand(dt, emat)
    wst = (xdt * _expand(te, emat)).astype(bf16)

    eye = _eye_bf16()
    a_cs_t = _transpose_f32(a_cs, eye)
    causal = lax.broadcasted_iota(jnp.int32, (L, L), 0) >= lax.broadcasted_iota(jnp.int32, (L, L), 1)
    intra = _intra_chunk(cm, bm, xdt, a_cs, a_cs_t, causal)

    y_parts = []
    for g in range(SSD_GROUPS):
        gs = slice(g * GROUP_W, (g + 1) * GROUP_W)
        cg = cm[:, g * SSD_STATE:(g + 1) * SSD_STATE].astype(bf16)
        bg = bm[:, g * SSD_STATE:(g + 1) * SSD_STATE].astype(bf16)
        h_g = hT[:, gs]
        y_inter = _dot(cg, h_g.astype(bf16)) * ea_e[:, gs]
        bg_t = _dot_nt(eye, bg).astype(bf16)
        hT[:, gs] = h_g * ea_e[L - 1:L, gs] + _dot(bg_t, wst[:, gs])
        y_parts.append(jnp.concatenate(intra[g], axis=1) + y_inter)
    y = jnp.concatenate(y_parts, axis=1) + xs * dsk_ref[...]
    y = y * _silu(z_ref[...])
    y_ref[...] = _rms(y, g_ref[...]).astype(y_ref.dtype)

    @pl.when(c == n_chunks - 1)
    def _():
        hout_ref[...] = hT[...].T


def _ssd_weight_specs():
    return [_const_spec((SSD_CONV, CONV_DIM)), _const_spec((1, CONV_DIM)), _const_spec((1, LANES)),
            _const_spec((1, LANES)), _const_spec((1, D_SSD)), _const_spec((1, D_SSD)),
            _const_spec((LANES, D_SSD))]


def _ssd_prompt(z, xbc, dt, ssd_w, *, batch, L, n_chunks):
    hp = SSD_HEADS * SSD_HEAD_DIM
    rowspec = lambda n: pl.BlockSpec((L, n), lambda b, c: (b * n_chunks + c, 0))
    return pl.pallas_call(
        functools.partial(_ssd_prompt_body, L=L, n_chunks=n_chunks),
        grid=(batch, n_chunks),
        in_specs=[rowspec(D_SSD), rowspec(CONV_DIM), rowspec(LANES)] + _ssd_weight_specs(),
        out_specs=[rowspec(D_SSD), pl.BlockSpec((None, hp, SSD_STATE), lambda b, c: (b, 0, 0))],
        out_shape=[jax.ShapeDtypeStruct((batch * n_chunks * L, D_SSD), bf16),
                   jax.ShapeDtypeStruct((batch, hp, SSD_STATE), f32)],
        scratch_shapes=[pltpu.VMEM((SUBLANES + L, CONV_DIM), f32), pltpu.VMEM((SSD_STATE, hp), f32)],
        compiler_params=_cparams("parallel", "arbitrary"),
        name="ssd_prompt",
    )(z, xbc, dt, *ssd_w)


def _ssd_sample_body(z_ref, xbc_ref, dt_ref, pre_ref, h0_ref, cw_ref, cb_ref, dtb_ref, alog_ref, dsk_ref, g_ref,
                     e_ref, *rest, n_seq, ts):
    y_ref, hout_ref, cbuf, yint = rest[-4:]
    R = SAMPLE_ROWS
    L = n_seq * R
    first = R - ts

    cbuf[0:SUBLANES, :] = jnp.zeros((SUBLANES, CONV_DIM), f32)
    cbuf[SUBLANES:SUBLANES + L, :] = xbc_ref[...]
    for s in range(n_seq):
        r0 = SUBLANES + s * R + first - N_PRE
        cbuf[r0:r0 + N_PRE, :] = pre_ref[s]
    act = _silu(_conv_taps(cbuf, cw_ref[...], cb_ref[...], L))
    xs = act[:, :D_SSD]
    bm = act[:, D_SSD:D_SSD + SSD_GROUPS * SSD_STATE]
    cm = act[:, D_SSD + SSD_GROUPS * SSD_STATE:]

    slot = jnp.bitwise_and(lax.broadcasted_iota(jnp.int32, (L, LANES), 0), R - 1)
    dt = jnp.where(slot >= first, jax.nn.softplus(dt_ref[...] + dtb_ref[...]), 0.0)
    a = dt * (-jnp.exp(alog_ref[...]))
    a_cs = a
    a_sf = a
    s = 1
    while s < R:
        a_cs = a_cs + jnp.where(slot >= s, pltpu.roll(a_cs, s, 0), 0.0)
        a_sf = a_sf + jnp.where(slot < R - s, pltpu.roll(a_sf, L - s, 0), 0.0)
        s *= 2
    ea = jnp.exp(a_cs)
    te = jnp.exp(a_sf - a)
    cd = jnp.exp(a_cs + a_sf - a)

    emat = e_ref[...]
    ea_e = _expand(ea, emat)
    xdt = xs * _expand(dt, emat)
    w_t = (xdt * _expand(te, emat)).T
    cd_t = _expand(cd, emat).T

    eye = _eye_bf16()
    a_cs_t = _transpose_f32(a_cs, eye)
    li = lax.broadcasted_iota(jnp.int32, (L, L), 0)
    si = lax.broadcasted_iota(jnp.int32, (L, L), 1)
    mask = jnp.logical_and(li >= si, jnp.bitwise_and(li, -R) == jnp.bitwise_and(si, -R))
    intra = _intra_chunk(cm, bm, xdt, a_cs, a_cs_t, mask)

    col = lax.broadcasted_iota(jnp.int32, (GROUP_W, L), 1)
    cmb = cm.astype(bf16)
    bmb = bm.astype(bf16)
    for s in range(n_seq):
        rs = slice(s * R, (s + 1) * R)
        mine = jnp.logical_and(col >= s * R, col < (s + 1) * R)
        for g in range(SSD_GROUPS):
            gs = slice(g * GROUP_W, (g + 1) * GROUP_W)
            ns = slice(g * SSD_STATE, (g + 1) * SSD_STATE)
            h0 = h0_ref[s, gs, :]
            yint[rs, gs] = _dot_nt(cmb[rs, ns], h0.astype(bf16))
            upd = _dot(jnp.where(mine, w_t[gs, :], 0.0).astype(bf16), bmb[:, ns])
            hout_ref[s, gs, :] = h0 * cd_t[gs, s * R:s * R + 1] + upd

    y = jnp.concatenate(intra[0] + intra[1], axis=1) + yint[...] * ea_e + xs * dsk_ref[...]
    y = y * _silu(z_ref[...])
    y_ref[...] = _rms(y, g_ref[...]).astype(y_ref.dtype)


def _ssd_sample(z, xbc, dt, pre, h0, ssd_w, prev_stack, *, layer, depth, batch, ts, n_seq):
    hp = SSD_HEADS * SSD_HEAD_DIM
    L = n_seq * SAMPLE_ROWS
    assert batch % n_seq == 0
    rowspec = lambda n: pl.BlockSpec((L, n), lambda i: (i, 0))
    seqspec = lambda a, b: pl.BlockSpec((None, n_seq, a, b), lambda i: (layer, i, 0, 0))
    n_prev = len(prev_stack)
    n_in = 5 + len(_ssd_weight_specs())
    return pl.pallas_call(
        functools.partial(_ssd_sample_body, n_seq=n_seq, ts=ts),
        grid=(batch // n_seq,),
        in_specs=[rowspec(D_SSD), rowspec(CONV_DIM), rowspec(LANES), seqspec(N_PRE, CONV_DIM),
                  seqspec(hp, SSD_STATE)] + _ssd_weight_specs() + [pl.BlockSpec(memory_space=pl.ANY)] * n_prev,
        out_specs=[rowspec(D_SSD), seqspec(hp, SSD_STATE)],
        out_shape=[jax.ShapeDtypeStruct((batch * SAMPLE_ROWS, D_SSD), bf16),
                   jax.ShapeDtypeStruct((depth, batch, hp, SSD_STATE), f32)],
        input_output_aliases={n_in + j: 1 + j for j in range(n_prev)},
        scratch_shapes=[pltpu.VMEM((SUBLANES + L, CONV_DIM), f32), pltpu.VMEM((L, D_SSD), f32)],
        compiler_params=_cparams("parallel"),
        name="ssd_sample",
    )(z, xbc, dt, pre, h0, *ssd_w, *prev_stack)


def _lambda(lamp_ref, lam_init):
    lp = lamp_ref[...]
    t1 = jnp.sum(lp[0:1, :] * lp[1:2, :], axis=-1, keepdims=True)
    t2 = jnp.sum(lp[2:3, :] * lp[3:4, :], axis=-1, keepdims=True)
    return jnp.exp(t1) - jnp.exp(t2) + lam_init


def _mask_maps(qr, lane):
    zero = jnp.zeros_like(qr)
    return jnp.where(lane < ATTN_HEAD_DIM, qr, zero), jnp.where(lane >= ATTN_HEAD_DIM, qr, zero)


def _pattn_body(slopes_ref, qT_ref, k_ref, vT_ref, lamp_ref, g_ref, o_ref,
                q4T, sb_ref, pT_ref, m_ref, l_ref, al_ref, accT, *, tq, lam_init):
    kv = pl.program_id(1)
    qi = pl.program_id(2)
    n_grp = 2 * ATTN_REP
    hd = ATTN_HEAD_DIM

    zero = jnp.zeros((hd, tq), bf16)
    for r in range(ATTN_REP):
        c1 = slice(r * tq, (r + 1) * tq)
        c2 = slice((ATTN_REP + r) * tq, (ATTN_REP + r + 1) * tq)
        q4T[0:hd, c1] = qT_ref[r * LANES:r * LANES + hd, :]
        q4T[hd:2 * hd, c1] = zero
        q4T[0:hd, c2] = zero
        q4T[hd:2 * hd, c2] = qT_ref[r * LANES + hd:(r + 1) * LANES, :]
    m_ref[...] = jnp.full_like(m_ref, NEG)
    l_ref[...] = jnp.zeros_like(l_ref)
    accT[...] = jnp.zeros_like(accT)

    rel = (lax.broadcasted_iota(jnp.int32, (tq, tq), 1) - lax.broadcasted_iota(jnp.int32, (tq, tq), 0)).astype(f32)
    slope = [slopes_ref[kv * ATTN_REP + r] for r in range(ATTN_REP)]
    for r in range(ATTN_REP):
        sb_ref[r] = slope[r] * rel

    def step(j, masked):
        start = pl.multiple_of(j * tq, tq)
        kblk = k_ref[pl.ds(start, tq), :]
        blk_dist = ((qi - j) * tq).astype(f32)
        for g in range(n_grp):
            cs = slice(g * tq, (g + 1) * tq)
            s = _dot(kblk, q4T[:, cs]) - sb_ref[g % ATTN_REP]
            if masked:
                s = jnp.where(rel >= 0.0, s, NEG)
            off = slope[g % ATTN_REP] * blk_dist
            m_old = m_ref[:, cs]
            m_new = jnp.maximum(m_old, jnp.max(s, axis=0, keepdims=True) - off)
            alpha = jnp.exp(m_old - m_new)
            p = jnp.exp(s - (m_new + off))
            l_ref[:, cs] = alpha * l_ref[:, cs] + jnp.sum(p, axis=0, keepdims=True)
            m_ref[:, cs] = m_new
            al_ref[:, cs] = alpha
            pT_ref[:, cs] = p.astype(bf16)
        accT[...] = accT[...] * al_ref[...] + _dot(vT_ref[j], pT_ref[...])

    def loop_body(j, carry):
        step(j, False)
        return carry

    lax.fori_loop(0, qi, loop_body, 0)
    step(qi, True)

    lam = _lambda(lamp_ref, lam_init)
    for r in range(ATTN_REP):
        c1 = slice(r * tq, (r + 1) * tq)
        c2 = slice((ATTN_REP + r) * tq, (ATTN_REP + r + 1) * tq)
        oT = accT[:, c1] / l_ref[:, c1] - lam * (accT[:, c2] / l_ref[:, c2])
        oT = oT * lax.rsqrt(jnp.mean(oT * oT, axis=0, keepdims=True) + EPS) * g_ref[...] * (1.0 - lam_init)
        o_ref[:, r * LANES:(r + 1) * LANES] = oT.T.astype(o_ref.dtype)


def _prompt_attention(slopes, qT, kb, vT, lamp, g_col, *, batch, T, tq, lam_init):
    nq = T // tq
    cols = 2 * ATTN_REP * tq
    return pl.pallas_call(
        functools.partial(_pattn_body, tq=tq, lam_init=lam_init),
        grid=(batch, ATTN_KV_HEADS, nq),
        in_specs=[pl.BlockSpec(memory_space=pltpu.SMEM),
                  pl.BlockSpec((ATTN_REP * LANES, tq), lambda b, h, i: (h, b * nq + i)),
                  pl.BlockSpec((T, LANES), lambda b, h, i: (b, h)),
                  pl.BlockSpec((nq, LANES, tq), lambda b, h, i: (b, h, 0)),
                  _const_spec((4, ATTN_HEAD_DIM)), _const_spec((ATTN_V_DIM, 1))],
        out_specs=pl.BlockSpec((tq, ATTN_REP * LANES), lambda b, h, i: (b * nq + i, h)),
        out_shape=jax.ShapeDtypeStruct((batch * T, D_ATTN), bf16),
        scratch_shapes=[pltpu.VMEM((LANES, cols), bf16), pltpu.VMEM((ATTN_REP, tq, tq), f32),
                        pltpu.VMEM((tq, cols), bf16), pltpu.VMEM((1, cols), f32), pltpu.VMEM((1, cols), f32),
                        pltpu.VMEM((1, cols), f32), pltpu.VMEM((ATTN_V_DIM, cols), f32)],
        compiler_params=_cparams("parallel", "parallel", "arbitrary"),
        name="prompt_attention",
    )(slopes, qT, kb, vT, lamp, g_col)


def _sattn_body(pt_ref, slopes_ref, q_ref, kn_ref, vn_ref, *rest, n_pages, page, ts, lam_init):
    k_pages = rest[:n_pages]
    v_pages = rest[n_pages:2 * n_pages]
    lamp_ref, g_ref, o_ref = rest[2 * n_pages:]
    del pt_ref
    past = n_pages * page
    R = SAMPLE_ROWS
    first = R - ts
    width = past + page

    row = lax.broadcasted_iota(jnp.int32, (R, width), 0)
    col = lax.broadcasted_iota(jnp.int32, (R, width), 1)
    is_new = col >= past
    kpos = col - jnp.where(is_new, first, 0)
    dist = (past + row - first - kpos).astype(f32)
    ok = jnp.logical_and(dist >= 0.0, jnp.logical_or(jnp.logical_not(is_new), col >= past + first))
    lane = lax.broadcasted_iota(jnp.int32, (R, LANES), 1)
    lam = _lambda(lamp_ref, lam_init)
    q = q_ref[...].astype(f32)
    zpad = jnp.zeros((page - R, LANES), bf16)

    for h in range(ATTN_KV_HEADS):
        groups = []
        for r in range(ATTN_REP):
            c0 = (h * ATTN_REP + r) * LANES
            groups.append(_mask_maps(q[:, c0:c0 + LANES], lane))
        q32 = jnp.concatenate([groups[0][0], groups[1][0], groups[0][1], groups[1][1]], axis=0).astype(bf16)
        hs = slice(h * LANES, (h + 1) * LANES)
        head_rows = pl.ds(h, page, stride=ATTN_KV_HEADS)
        kcat = jnp.concatenate([kp[head_rows, :].astype(bf16) for kp in k_pages]
                               + [kn_ref[:, hs].astype(bf16), zpad], axis=0)
        vcat = jnp.concatenate([vp[head_rows, :].astype(bf16) for vp in v_pages]
                               + [vn_ref[:, hs].astype(bf16), zpad], axis=0)
        s = _dot_nt(q32, kcat)
        ps = []
        ls = []
        for g in range(2 * ATTN_REP):
            slope = slopes_ref[h * ATTN_REP + g % ATTN_REP]
            sg = jnp.where(ok, s[g * R:(g + 1) * R, :] - slope * dist, NEG)
            p = jnp.exp(sg - jnp.max(sg, axis=-1, keepdims=True))
            ls.append(jnp.sum(p, axis=-1, keepdims=True))
            ps.append(p.astype(bf16))
        acc = _dot(jnp.concatenate(ps, axis=0), vcat)
        for r in range(ATTN_REP):
            g1, g2 = r, ATTN_REP + r
            o = acc[g1 * R:(g1 + 1) * R, :] / ls[g1] - lam * (acc[g2 * R:(g2 + 1) * R, :] / ls[g2])
            o = _rms(o, g_ref[...]) * (1.0 - lam_init)
            c0 = (h * ATTN_REP + r) * LANES
            o_ref[:, c0:c0 + LANES] = o.astype(o_ref.dtype)


def _sample_attention(page_table, slopes, q, kn, vn, cache_k, cache_v, lamp, g, *, layer, ts, lam_init):
    bs, n_pages = page_table.shape
    page = cache_k.shape[2] // ATTN_KV_HEADS
    R = SAMPLE_ROWS

    def page_spec(p):
        return pl.BlockSpec((None, None, page * ATTN_KV_HEADS, LANES), lambda b, pt: (layer, pt[b, p], 0, 0))

    seq = lambda n: pl.BlockSpec((R, n), lambda b, pt: (b, 0))
    const = lambda shape: pl.BlockSpec(shape, lambda b, pt: (0,) * len(shape), pipeline_mode=pl.Buffered(1))
    grid_spec = pltpu.PrefetchScalarGridSpec(
        num_scalar_prefetch=1,
        grid=(bs,),
        in_specs=[pl.BlockSpec(memory_space=pltpu.SMEM), seq(Q_DIM), seq(K_DIM), seq(V_DIM)]
        + [page_spec(p) for p in range(n_pages)] * 2
        + [const((4, ATTN_HEAD_DIM)), const((1, ATTN_V_DIM))],
        out_specs=seq(D_ATTN),
    )
    return pl.pallas_call(
        functools.partial(_sattn_body, n_pages=n_pages, page=page, ts=ts, lam_init=lam_init),
        grid_spec=grid_spec,
        out_shape=jax.ShapeDtypeStruct((bs * R, D_ATTN), bf16),
        compiler_params=_cparams("parallel"),
        name="sample_attention",
    )(page_table, slopes, q, kn, vn, *([cache_k] * n_pages), *([cache_v] * n_pages), lamp, g)


def _alibi_slopes():
    return jnp.asarray(np.array([2.0 ** (-8.0 * (i + 1) / ATTN_HEADS) for i in range(ATTN_HEADS)], np.float32))


def _expand_matrix():
    e = np.zeros((LANES, D_SSD), np.float32)
    for h in range(SSD_HEADS):
        e[h, h * SSD_HEAD_DIM:(h + 1) * SSD_HEAD_DIM] = 1.0
    return jnp.asarray(e, bf16)


def _pad_lanes(v):
    return jnp.pad(v, (0, LANES - v.shape[0]))[None, :]


def kernel(x_prompt, x_sample, cache_k, cache_v, state_ssm, state_conv, page_table, norm_mix, w_in, conv_w, conv_b, dt_bias, A_log, D_skip, ssd_norm, lambda_q1, lambda_k1, lambda_q2, lambda_k2, attn_subln, w_out, norm_mlp, w_up, w_down, norm_final):
    depth = w_in.shape[0]
    bp, T, _ = x_prompt.shape
    bs, ts, _ = x_sample.shape
    R = SAMPLE_ROWS
    assert T % SSD_CHUNK == 0 and N_PRE <= ts <= R - N_PRE and ts % SSD_CHUNK != 0
    mp = bp * T
    hp = SSD_HEADS * SSD_HEAD_DIM

    slopes = _alibi_slopes()
    emat = _expand_matrix()
    gf = norm_final[None, :]

    xp = x_prompt.reshape(mp, D_MODEL)
    xs = jnp.pad(x_sample, ((0, 0), (R - ts, 0), (0, 0))).reshape(bs * R, D_MODEL)
    state4 = state_ssm.reshape(depth, bs, hp, SSD_STATE)

    o0 = D_SSD
    o1 = o0 + CONV_DIM
    o2 = o1 + SSD_HEADS
    o3 = o2 + Q_DIM
    o4 = o3 + K_DIM

    n_pool = cache_k.shape[1]
    page_rows = cache_k.shape[2] * ATTN_KV_HEADS
    cache_k2 = cache_k.reshape(depth, n_pool, page_rows, LANES)
    cache_v2 = cache_v.reshape(depth, n_pool, page_rows, LANES)
    tk = 256

    outs = {k: [] for k in ("hp", "cp", "ks", "vs", "cs")}
    kv_stacks = ()
    hs_stack = ()
    for l in range(depth):
        lam_init = 0.8 - 0.6 * math.exp(-0.3 * l)
        wl = w_in[l]
        w_parts = (wl[:, :o0].astype(bf16), wl[:, o0:o1].astype(bf16),
                   jnp.pad(wl[:, o1:o2], ((0, 0), (0, LANES - SSD_HEADS))).astype(bf16),
                   wl[:, o2:o3].astype(bf16), wl[:, o3:o4].astype(bf16), wl[:, o4:].astype(bf16))
        g_mix = norm_mix[l][None, :]
        ssd_w = (conv_w[l], conv_b[l][None, :], _pad_lanes(dt_bias[l]), _pad_lanes(A_log[l]),
                 jnp.repeat(D_skip[l], SSD_HEAD_DIM)[None, :], ssd_norm[l][None, :], emat)
        lamp = jnp.stack([lambda_q1[l], lambda_k1[l], lambda_q2[l], lambda_k2[l]])
        g_sub = attn_subln[l][None, :]
        woy = w_out[l, :D_SSD].astype(bf16)
        woo = w_out[l, D_SSD:].astype(bf16)
        mlp_w = (norm_mlp[l][None, :], w_up[l].astype(bf16), w_down[l].astype(bf16), gf)
        final = l == depth - 1

        z, xbc, dt, qT, kb, vT, k_st, v_st = _in_proj_prompt(xp, g_mix, w_parts, kv_stacks, layer=l, depth=depth,
                                                             tm=512, tk=tk)
        kv_stacks = (k_st, v_st)
        y, h_new = _ssd_prompt(z, xbc, dt, ssd_w, batch=bp, L=SSD_CHUNK, n_chunks=T // SSD_CHUNK)
        o = _prompt_attention(slopes, qT, kb, vT, lamp, g_sub.reshape(ATTN_V_DIM, 1), batch=bp, T=T, tq=tk,
                              lam_init=lam_init)
        xp = _out_mlp(y, o, xp, woy, woo, *mlp_w, 512, final)
        outs["hp"].append(h_new.reshape(bp, SSD_HEADS, SSD_HEAD_DIM, SSD_STATE))
        outs["cp"].append(xbc.reshape(bp, T, CONV_DIM)[:, T - N_PRE:, :])

        z, xbc, dt, q, k, v = _in_proj_sample(xs, g_mix, w_parts, 512)
        y, hs_new = _ssd_sample(z, xbc, dt, state_conv, state4, ssd_w, hs_stack, layer=l, depth=depth, batch=bs,
                                ts=ts, n_seq=16)
        hs_stack = (hs_new,)
        o = _sample_attention(page_table, slopes, q, k, v, cache_k2, cache_v2, lamp, g_sub,
                              layer=l, ts=ts, lam_init=lam_init)
        xs = _out_mlp(y, o, xs, woy, woo, *mlp_w, 512, final)
        outs["ks"].append(k.reshape(bs, R, ATTN_KV_HEADS, 2 * ATTN_HEAD_DIM)[:, R - ts:])
        outs["vs"].append(v.reshape(bs, R, ATTN_KV_HEADS, ATTN_V_DIM)[:, R - ts:])
        outs["cs"].append(xbc.reshape(bs, R, CONV_DIM)[:, R - N_PRE:])

    y_prompt = xp.reshape(bp, T, D_MODEL)
    y_sample = xs.reshape(bs, R, D_MODEL)[:, R - ts:]
    new_k_prompt = kv_stacks[0].reshape(depth, bp, T, ATTN_KV_HEADS, 2 * ATTN_HEAD_DIM)
    new_v_prompt = kv_stacks[1].reshape(depth, bp, T, ATTN_KV_HEADS, ATTN_V_DIM)
    new_ssm_sample = hs_stack[0].reshape(depth, bs, SSD_HEADS, SSD_HEAD_DIM, SSD_STATE)
    st = lambda key: jnp.stack(outs[key])
    return (y_prompt, y_sample, new_k_prompt, new_v_prompt, st("hp"), st("cp"), st("ks"), st("vs"),
            new_ssm_sample, st("cs"))
```

```python
import functools
import math

import numpy as np
import jax
import jax.numpy as jnp
from jax import lax
from jax.experimental import pallas as pl
from jax.experimental.pallas import tpu as pltpu

f32 = jnp.float32
bf16 = jnp.bfloat16

D_MODEL = 1024
D_SSD = 1024
D_ATTN = 1024
SSD_HEAD_DIM = 64
SSD_HEADS = 16
SSD_GROUPS = 2
SSD_STATE = 128
SSD_CONV = 4
SSD_CHUNK = 128
CONV_DIM = D_SSD + 2 * SSD_GROUPS * SSD_STATE
ATTN_HEAD_DIM = 64
ATTN_V_DIM = 128
ATTN_HEADS = 8
ATTN_KV_HEADS = 4
ATTN_REP = 2
Q_DIM = ATTN_HEADS * 2 * ATTN_HEAD_DIM
K_DIM = ATTN_KV_HEADS * 2 * ATTN_HEAD_DIM
V_DIM = ATTN_KV_HEADS * ATTN_V_DIM
D_FF = 4 * D_MODEL
EPS = 1e-6
NEG = -1e30
LOG2E = math.log2(math.e)

LANES = 128
SUBLANES = 8
VMEM_LIMIT_BYTES = 56 * 1024 * 1024

SAMPLE_ROWS = SUBLANES
GROUP_W = D_SSD // SSD_GROUPS
HEADS_PER_GROUP = SSD_HEADS // SSD_GROUPS
N_PRE = SSD_CONV - 1
VT_ROWS = ATTN_V_DIM + 16


def _row_tile(m, want):
    t = min(want, m)
    while m % t:
        t -= SUBLANES
    return t


def _cparams(*sem):
    return pltpu.CompilerParams(dimension_semantics=sem, vmem_limit_bytes=VMEM_LIMIT_BYTES)


def _const_spec(shape):
    nd = len(shape)
    return pl.BlockSpec(shape, lambda *_: (0,) * nd, pipeline_mode=pl.Buffered(1))


def _rms(x, g):
    return x * lax.rsqrt(jnp.mean(x * x, axis=-1, keepdims=True) + EPS) * g


def _silu(x):
    return x * jax.nn.sigmoid(x)


def _dot(a, b):
    return jnp.dot(a, b, preferred_element_type=f32)


def _dot_nt(a, b):
    return lax.dot_general(a, b, (((1,), (1,)), ((), ())), preferred_element_type=f32)


IN_WIDTHS = (D_SSD, CONV_DIM, LANES, Q_DIM, K_DIM, V_DIM)


def _inproj_common(x_ref, g_ref, wz, wxbc, wdt, wq, z_o, xbc_o, dt_o):
    xb = _rms(x_ref[...], g_ref[...]).astype(bf16)
    z_o[...] = _dot(xb, wz[...])
    xbc_o[...] = _dot(xb, wxbc[...])
    dt_o[...] = _dot(xb, wdt[...])
    return xb, _dot(xb, wq[...]) * (ATTN_HEAD_DIM ** -0.5)


def _inproj_sample_body(x_ref, g_ref, wz, wxbc, wdt, wq, wk, wv, z_o, xbc_o, dt_o, q_o, k_o, v_o):
    xb, q = _inproj_common(x_ref, g_ref, wz, wxbc, wdt, wq, z_o, xbc_o, dt_o)
    q_o[...] = q.astype(bf16)
    k_o[...] = _dot(xb, wk[...])
    v_o[...] = _dot(xb, wv[...])


def _inproj_prompt_body(x_ref, g_ref, wz, wxbc, wdt, wq, wk, wv, *rest, tk, n_prev):
    z_o, xbc_o, dt_o, qT_o, kb_o, vT_o, kst_o, vst_o = rest[n_prev:]
    xb, q = _inproj_common(x_ref, g_ref, wz, wxbc, wdt, wq, z_o, xbc_o, dt_o)
    tm = q.shape[0]
    qT_o[...] = (q * LOG2E).T.astype(bf16)
    k = _dot(xb, wk[...])
    v = _dot(xb, wv[...])
    kb_o[...] = k.astype(bf16)
    vT = v.T.astype(bf16)
    ones = jnp.ones((VT_ROWS - ATTN_V_DIM, tk), bf16)
    for i in range(tm // tk):
        for h in range(ATTN_KV_HEADS):
            vT_o[i, h, 0:ATTN_V_DIM, :] = vT[h * ATTN_V_DIM:(h + 1) * ATTN_V_DIM, i * tk:(i + 1) * tk]
            vT_o[i, h, ATTN_V_DIM:VT_ROWS, :] = ones
    for h in range(ATTN_KV_HEADS):
        hs = slice(h * LANES, (h + 1) * LANES)
        kst_o[pl.ds(h, tm, stride=ATTN_KV_HEADS), :] = k[:, hs]
        vst_o[pl.ds(h, tm, stride=ATTN_KV_HEADS), :] = v[:, hs]


def _in_proj_sample(x, g, w, tm):
    m = x.shape[0]
    tm = _row_tile(m, tm)
    row = lambda n: pl.BlockSpec((tm, n), lambda i: (i, 0))
    out_dt = (f32, f32, f32, bf16, f32, f32)
    return pl.pallas_call(
        _inproj_sample_body,
        grid=(m // tm,),
        in_specs=[row(D_MODEL), _const_spec((1, D_MODEL))] + [_const_spec((D_MODEL, n)) for n in IN_WIDTHS],
        out_specs=[row(n) for n in IN_WIDTHS],
        out_shape=[jax.ShapeDtypeStruct((m, n), d) for n, d in zip(IN_WIDTHS, out_dt)],
        compiler_params=_cparams("parallel"),
        name="in_proj_sample",
    )(x, g, *w)


def _in_proj_prompt(x, g, w, prev_stacks, *, layer, depth, tm, tk):
    m = x.shape[0]
    tm = _row_tile(m, tm)
    assert tm % tk == 0
    row = lambda n: pl.BlockSpec((tm, n), lambda i: (i, 0))
    n_prev = len(prev_stacks)
    stack_spec = pl.BlockSpec((None, tm * ATTN_KV_HEADS, LANES), lambda i: (layer, i, 0))
    stack_shape = jax.ShapeDtypeStruct((depth, m * ATTN_KV_HEADS, LANES), f32)
    n_in = 2 + len(IN_WIDTHS)
    return pl.pallas_call(
        functools.partial(_inproj_prompt_body, tk=tk, n_prev=n_prev),
        grid=(m // tm,),
        in_specs=[row(D_MODEL), _const_spec((1, D_MODEL))] + [_const_spec((D_MODEL, n)) for n in IN_WIDTHS]
        + [pl.BlockSpec(memory_space=pl.ANY)] * n_prev,
        out_specs=[row(D_SSD), row(CONV_DIM), row(LANES),
                   pl.BlockSpec((Q_DIM, tm), lambda i: (0, i)), row(K_DIM),
                   pl.BlockSpec((tm // tk, ATTN_KV_HEADS, VT_ROWS, tk), lambda i: (i, 0, 0, 0)),
                   stack_spec, stack_spec],
        out_shape=[jax.ShapeDtypeStruct((m, D_SSD), f32), jax.ShapeDtypeStruct((m, CONV_DIM), f32),
                   jax.ShapeDtypeStruct((m, LANES), f32), jax.ShapeDtypeStruct((Q_DIM, m), bf16),
                   jax.ShapeDtypeStruct((m, K_DIM), bf16), jax.ShapeDtypeStruct((m // tk, ATTN_KV_HEADS, VT_ROWS, tk), bf16),
                   stack_shape, stack_shape],
        input_output_aliases={n_in + j: 6 + j for j in range(n_prev)},
        compiler_params=_cparams("parallel"),
        name="in_proj_prompt",
    )(x, g, *w, *prev_stacks)


def _outmlp_body(y_ref, o_ref, x_ref, woy, woo, g_ref, wup, wdn, gf_ref, out_ref, *, final, ff_chunk):
    x1 = x_ref[...] + _dot(y_ref[...], woy[...]) + _dot(o_ref[...], woo[...])
    h = _rms(x1, g_ref[...]).astype(bf16)
    mlp = None
    for j in range(D_FF // ff_chunk):
        u = _dot(h, wup[:, j * ff_chunk:(j + 1) * ff_chunk])
        u = jnp.square(jnp.maximum(u, 0.0)).astype(bf16)
        d = _dot(u, wdn[j * ff_chunk:(j + 1) * ff_chunk, :])
        mlp = d if mlp is None else mlp + d
    x2 = x1 + mlp
    out_ref[...] = _rms(x2, gf_ref[...]) if final else x2


def _out_mlp(y, o, x, woy, woo, g, wup, wdn, gf, tm, final):
    m = x.shape[0]
    tm = _row_tile(m, tm)
    row = lambda n: pl.BlockSpec((tm, n), lambda i: (i, 0))
    return pl.pallas_call(
        functools.partial(_outmlp_body, final=final, ff_chunk=1024),
        grid=(m // tm,),
        in_specs=[row(D_SSD), row(D_ATTN), row(D_MODEL),
                  _const_spec((D_SSD, D_MODEL)), _const_spec((D_ATTN, D_MODEL)), _const_spec((1, D_MODEL)),
                  _const_spec((D_MODEL, D_FF)), _const_spec((D_FF, D_MODEL)), _const_spec((1, D_MODEL))],
        out_specs=row(D_MODEL),
        out_shape=jax.ShapeDtypeStruct((m, D_MODEL), f32),
        compiler_params=_cparams("parallel"),
        name="out_mlp",
    )(y, o, x, woy, woo, g, wup, wdn, gf)


N_PIECES = 3


def _split3(x):
    p1 = x.astype(bf16)
    r1 = x - p1.astype(f32)
    p2 = r1.astype(bf16)
    p3 = (r1 - p2.astype(f32)).astype(bf16)
    return jnp.concatenate([p1, p2, p3], axis=1)


def _expand(v, emat3):
    return _dot(_split3(v), emat3)


def _eye_bf16(reps):
    ri = lax.broadcasted_iota(jnp.int32, (LANES, reps * LANES), 0)
    ci = jnp.bitwise_and(lax.broadcasted_iota(jnp.int32, (LANES, reps * LANES), 1), LANES - 1)
    return jnp.where(ri == ci, 1.0, 0.0).astype(bf16)


def _transpose_f32(x, eye3):
    return _dot_nt(eye3, _split3(x))


def _conv_taps(cbuf, w, bias, L):
    conv = bias
    for i in range(SSD_CONV):
        off = SUBLANES - N_PRE + i
        conv = conv + w[i:i + 1, :] * cbuf[off:off + L, :]
    return conv


def _intra_chunk(cm, bm, xdt, a_cs, a_cs_t, mask):
    L = cm.shape[0]
    lane = lax.broadcasted_iota(jnp.int32, (L, LANES), 1)
    halves = (lane < SSD_HEAD_DIM, lane >= SSD_HEAD_DIM)
    out = []
    for g in range(SSD_GROUPS):
        cg = cm[:, g * SSD_STATE:(g + 1) * SSD_STATE].astype(bf16)
        bg = bm[:, g * SSD_STATE:(g + 1) * SSD_STATE].astype(bf16)
        cbm = _dot_nt(cg, bg)
        blocks = []
        for jp in range(HEADS_PER_GROUP // 2):
            pair = g * (HEADS_PER_GROUP // 2) + jp
            xp = xdt[:, pair * LANES:(pair + 1) * LANES]
            acc = None
            for k in range(2):
                h = 2 * pair + k
                seg = a_cs[:, h:h + 1] - a_cs_t[h:h + 1, :]
                dec = jnp.where(mask, jnp.exp(seg), 0.0)
                mh = (cbm * dec).astype(bf16)
                part = _dot(mh, jnp.where(halves[k], xp, 0.0).astype(bf16))
                acc = part if acc is None else acc + part
            blocks.append(acc)
        out.append(blocks)
    return out


def _ssd_prompt_body(z_ref, xbc_ref, dt_ref, cw_ref, cb_ref, dtb_ref, alog_ref, dsk_ref, g_ref, e_ref,
                     y_ref, hout_ref, cbuf, hT, *, L, n_chunks):
    c = pl.program_id(1)

    @pl.when(c == 0)
    def _():
        cbuf[0:SUBLANES, :] = jnp.zeros((SUBLANES, CONV_DIM), f32)
        hT[...] = jnp.zeros_like(hT)

    xraw = xbc_ref[...]
    cbuf[SUBLANES:SUBLANES + L, :] = xraw
    act = _silu(_conv_taps(cbuf, cw_ref[...], cb_ref[...], L))
    cbuf[0:SUBLANES, :] = xraw[L - SUBLANES:L, :]
    xs = act[:, :D_SSD]
    bm = act[:, D_SSD:D_SSD + SSD_GROUPS * SSD_STATE]
    cm = act[:, D_SSD + SSD_GROUPS * SSD_STATE:]

    rows = lax.broadcasted_iota(jnp.int32, (L, LANES), 0)
    dt = jax.nn.softplus(dt_ref[...] + dtb_ref[...])
    a = dt * (-jnp.exp(alog_ref[...]))
    a_cs = a
    s = 1
    while s < L:
        a_cs = a_cs + jnp.where(rows >= s, pltpu.roll(a_cs, s, 0), 0.0)
        s *= 2
    ea = jnp.exp(a_cs)
    te = jnp.exp(a_cs[L - 1:L, :] - a_cs)

    emat = e_ref[...]
    ea_e = _expand(ea, emat)
    xdt = xs * _expand(dt, emat)
    wst = (xdt * _expand(te, emat)).astype(bf16)

    eye = _eye_bf16(1)
    a_cs_t = _transpose_f32(a_cs, _eye_bf16(N_PIECES))
    causal = lax.broadcasted_iota(jnp.int32, (L, L), 0) >= lax.broadcasted_iota(jnp.int32, (L, L), 1)
    intra = _intra_chunk(cm, bm, xdt, a_cs, a_cs_t, causal)

    y_parts = []
    for g in range(SSD_GROUPS):
        gs = slice(g * GROUP_W, (g + 1) * GROUP_W)
        cg = cm[:, g * SSD_STATE:(g + 1) * SSD_STATE].astype(bf16)
        bg = bm[:, g * SSD_STATE:(g + 1) * SSD_STATE].astype(bf16)
        h_g = hT[:, gs]
        y_inter = _dot(cg, h_g.astype(bf16)) * ea_e[:, gs]
        bg_t = _dot_nt(eye, bg).astype(bf16)
        hT[:, gs] = h_g * ea_e[L - 1:L, gs] + _dot(bg_t, wst[:, gs])
        y_parts.append(jnp.concatenate(intra[g], axis=1) + y_inter)
    y = jnp.concatenate(y_parts, axis=1) + xs * dsk_ref[...]
    y = y * _silu(z_ref[...])
    y_ref[...] = _rms(y, g_ref[...]).astype(y_ref.dtype)

    @pl.when(c == n_chunks - 1)
    def _():
        hout_ref[...] = hT[...].T


def _ssd_weight_specs():
    return [_const_spec((SSD_CONV, CONV_DIM)), _const_spec((1, CONV_DIM)), _const_spec((1, LANES)),
            _const_spec((1, LANES)), _const_spec((1, D_SSD)), _const_spec((1, D_SSD)),
            _const_spec((N_PIECES * LANES, D_SSD))]


def _ssd_prompt(z, xbc, dt, ssd_w, *, batch, L, n_chunks):
    hp = SSD_HEADS * SSD_HEAD_DIM
    rowspec = lambda n: pl.BlockSpec((L, n), lambda b, c: (b * n_chunks + c, 0))
    return pl.pallas_call(
        functools.partial(_ssd_prompt_body, L=L, n_chunks=n_chunks),
        grid=(batch, n_chunks),
        in_specs=[rowspec(D_SSD), rowspec(CONV_DIM), rowspec(LANES)] + _ssd_weight_specs(),
        out_specs=[rowspec(D_SSD), pl.BlockSpec((None, hp, SSD_STATE), lambda b, c: (b, 0, 0))],
        out_shape=[jax.ShapeDtypeStruct((batch * n_chunks * L, D_SSD), bf16),
                   jax.ShapeDtypeStruct((batch, hp, SSD_STATE), f32)],
        scratch_shapes=[pltpu.VMEM((SUBLANES + L, CONV_DIM), f32), pltpu.VMEM((SSD_STATE, hp), f32)],
        compiler_params=_cparams("parallel", "arbitrary"),
        name="ssd_prompt",
    )(z, xbc, dt, *ssd_w)


def _ssd_sample_body(z_ref, xbc_ref, dt_ref, pre_ref, h0_ref, cw_ref, cb_ref, dtb_ref, alog_ref, dsk_ref, g_ref,
                     e_ref, *rest, n_seq, ts):
    y_ref, hout_ref, cbuf, yint = rest[-4:]
    R = SAMPLE_ROWS
    L = n_seq * R
    first = R - ts

    cbuf[0:SUBLANES, :] = jnp.zeros((SUBLANES, CONV_DIM), f32)
    cbuf[SUBLANES:SUBLANES + L, :] = xbc_ref[...]
    for s in range(n_seq):
        r0 = SUBLANES + s * R + first - N_PRE
        cbuf[r0:r0 + N_PRE, :] = pre_ref[s]
    act = _silu(_conv_taps(cbuf, cw_ref[...], cb_ref[...], L))
    xs = act[:, :D_SSD]
    bm = act[:, D_SSD:D_SSD + SSD_GROUPS * SSD_STATE]
    cm = act[:, D_SSD + SSD_GROUPS * SSD_STATE:]

    slot = jnp.bitwise_and(lax.broadcasted_iota(jnp.int32, (L, LANES), 0), R - 1)
    dt = jnp.where(slot >= first, jax.nn.softplus(dt_ref[...] + dtb_ref[...]), 0.0)
    a = dt * (-jnp.exp(alog_ref[...]))
    a_cs = a
    a_sf = a
    s = 1
    while s < R:
        a_cs = a_cs + jnp.where(slot >= s, pltpu.roll(a_cs, s, 0), 0.0)
        a_sf = a_sf + jnp.where(slot < R - s, pltpu.roll(a_sf, L - s, 0), 0.0)
        s *= 2
    ea = jnp.exp(a_cs)
    te = jnp.exp(a_sf - a)
    cd = jnp.exp(a_cs + a_sf - a)

    emat = e_ref[...]
    ea_e = _expand(ea, emat)
    xdt = xs * _expand(dt, emat)
    w_t = (xdt * _expand(te, emat)).T
    cd_t = _expand(cd, emat).T

    eye = _eye_bf16(1)
    a_cs_t = _transpose_f32(a_cs, _eye_bf16(N_PIECES))
    li = lax.broadcasted_iota(jnp.int32, (L, L), 0)
    si = lax.broadcasted_iota(jnp.int32, (L, L), 1)
    mask = jnp.logical_and(li >= si, jnp.bitwise_and(li, -R) == jnp.bitwise_and(si, -R))
    intra = _intra_chunk(cm, bm, xdt, a_cs, a_cs_t, mask)

    col = lax.broadcasted_iota(jnp.int32, (GROUP_W, L), 1)
    cmb = cm.astype(bf16)
    bmb = bm.astype(bf16)
    for s in range(n_seq):
        rs = slice(s * R, (s + 1) * R)
        mine = jnp.logical_and(col >= s * R, col < (s + 1) * R)
        for g in range(SSD_GROUPS):
            gs = slice(g * GROUP_W, (g + 1) * GROUP_W)
            ns = slice(g * SSD_STATE, (g + 1) * SSD_STATE)
            h0 = h0_ref[s, gs, :]
            yint[rs, gs] = _dot_nt(cmb[rs, ns], h0.astype(bf16))
            upd = _dot(jnp.where(mine, w_t[gs, :], 0.0).astype(bf16), bmb[:, ns])
            hout_ref[s, gs, :] = h0 * cd_t[gs, s * R:s * R + 1] + upd

    y = jnp.concatenate(intra[0] + intra[1], axis=1) + yint[...] * ea_e + xs * dsk_ref[...]
    y = y * _silu(z_ref[...])
    y_ref[...] = _rms(y, g_ref[...]).astype(y_ref.dtype)


def _ssd_sample(z, xbc, dt, pre, h0, ssd_w, prev_stack, *, layer, depth, batch, ts, n_seq):
    hp = SSD_HEADS * SSD_HEAD_DIM
    L = n_seq * SAMPLE_ROWS
    assert batch % n_seq == 0
    rowspec = lambda n: pl.BlockSpec((L, n), lambda i: (i, 0))
    seqspec = lambda a, b: pl.BlockSpec((None, n_seq, a, b), lambda i: (layer, i, 0, 0))
    n_prev = len(prev_stack)
    n_in = 5 + len(_ssd_weight_specs())
    return pl.pallas_call(
        functools.partial(_ssd_sample_body, n_seq=n_seq, ts=ts),
        grid=(batch // n_seq,),
        in_specs=[rowspec(D_SSD), rowspec(CONV_DIM), rowspec(LANES), seqspec(N_PRE, CONV_DIM),
                  seqspec(hp, SSD_STATE)] + _ssd_weight_specs() + [pl.BlockSpec(memory_space=pl.ANY)] * n_prev,
        out_specs=[rowspec(D_SSD), seqspec(hp, SSD_STATE)],
        out_shape=[jax.ShapeDtypeStruct((batch * SAMPLE_ROWS, D_SSD), bf16),
                   jax.ShapeDtypeStruct((depth, batch, hp, SSD_STATE), f32)],
        input_output_aliases={n_in + j: 1 + j for j in range(n_prev)},
        scratch_shapes=[pltpu.VMEM((SUBLANES + L, CONV_DIM), f32), pltpu.VMEM((L, D_SSD), f32)],
        compiler_params=_cparams("parallel"),
        name="ssd_sample",
    )(z, xbc, dt, pre, h0, *ssd_w, *prev_stack)


def _lambda(lamp_ref, lam_init):
    lp = lamp_ref[...]
    t1 = jnp.sum(lp[0:1, :] * lp[1:2, :], axis=-1, keepdims=True)
    t2 = jnp.sum(lp[2:3, :] * lp[3:4, :], axis=-1, keepdims=True)
    return jnp.exp(t1) - jnp.exp(t2) + lam_init


def _mask_maps(qr, lane):
    zero = jnp.zeros_like(qr)
    return jnp.where(lane < ATTN_HEAD_DIM, qr, zero), jnp.where(lane >= ATTN_HEAD_DIM, qr, zero)


def _pattn_body(slopes_ref, qT_ref, k_ref, vT_ref, lamp_ref, g_ref, o_ref,
                q4T, sb_ref, s0, s1, p0, p1, al0, al1, mx0, mx1, m_ref, accT, *, tq, lam_init, unroll):
    kv = pl.program_id(1)
    qi = pl.program_id(2)
    n_grp = 2 * ATTN_REP
    hd = ATTN_HEAD_DIM

    zero = jnp.zeros((hd, tq), bf16)
    for r in range(ATTN_REP):
        c1 = slice(r * tq, (r + 1) * tq)
        c2 = slice((ATTN_REP + r) * tq, (ATTN_REP + r + 1) * tq)
        q4T[0:hd, c1] = qT_ref[r * LANES:r * LANES + hd, :]
        q4T[hd:2 * hd, c1] = zero
        q4T[0:hd, c2] = zero
        q4T[hd:2 * hd, c2] = qT_ref[r * LANES + hd:(r + 1) * LANES, :]
    m_ref[...] = jnp.full_like(m_ref, NEG)
    accT[...] = jnp.zeros_like(accT)
    s_buf, p_buf, al_buf, mx_buf = (s0, s1), (p0, p1), (al0, al1), (mx0, mx1)
    al1[...] = jnp.ones_like(al1)
    p1[...] = jnp.zeros_like(p1)

    rel = (lax.broadcasted_iota(jnp.int32, (tq, tq), 1) - lax.broadcasted_iota(jnp.int32, (tq, tq), 0)).astype(f32)
    slope = [slopes_ref[kv * ATTN_REP + r] * LOG2E for r in range(ATTN_REP)]
    for r in range(ATTN_REP):
        sb_ref[r] = slope[r] * rel

    def scores(j, slot):
        start = pl.multiple_of(j * tq, tq)
        kblk = k_ref[pl.ds(start, tq), :]
        for g in range(n_grp):
            cs = slice(g * tq, (g + 1) * tq)
            s = _dot(kblk, q4T[:, cs]) - sb_ref[g % ATTN_REP]
            s_buf[slot][:, cs] = s
            mx_buf[slot][:, cs] = jnp.max(s, axis=0, keepdims=True)

    def softmax(j, slot, masked):
        blk_dist = ((qi - j) * tq).astype(f32)
        for g in range(n_grp):
            cs = slice(g * tq, (g + 1) * tq)
            s = s_buf[slot][:, cs]
            if masked:
                s = jnp.where(rel >= 0.0, s, NEG)
                mx = jnp.max(s, axis=0, keepdims=True)
            else:
                mx = mx_buf[slot][:, cs]
            off = slope[g % ATTN_REP] * blk_dist
            m_old = m_ref[:, cs]
            m_new = jnp.maximum(m_old, mx - off)
            m_ref[:, cs] = m_new
            al_buf[slot][:, cs] = jnp.exp2(m_old - m_new)
            p_buf[slot][:, cs] = jnp.exp2(s - (m_new + off)).astype(bf16)

    def values(j, slot):
        accT[...] = accT[...] * al_buf[slot][...] + _dot(vT_ref[j], p_buf[slot][...])

    def stage(j, slot):
        scores(j + 1, 1 - slot)
        softmax(j, slot, False)
        values(jnp.maximum(j - 1, 0), 1 - slot)

    def last(slot):
        softmax(qi, slot, True)
        values(jnp.maximum(qi - 1, 0), 1 - slot)
        values(qi, slot)

    scores(0, 0)

    shift = unroll.bit_length() - 1

    def group_body(i, carry):
        for u in range(unroll):
            stage(unroll * i + u, u % 2)
        return carry

    lax.fori_loop(0, jnp.right_shift(qi, shift), group_body, 0)
    rem = jnp.bitwise_and(qi, unroll - 1)
    base = qi - rem
    for u in range(unroll - 1):
        @pl.when(rem > u)
        def _():
            stage(base + u, u % 2)

    for parity in range(2):
        @pl.when(jnp.bitwise_and(qi, 1) == parity)
        def _():
            last(parity)

    lam = _lambda(lamp_ref, lam_init)
    for r in range(ATTN_REP):
        c1 = slice(r * tq, (r + 1) * tq)
        c2 = slice((ATTN_REP + r) * tq, (ATTN_REP + r + 1) * tq)
        nv = ATTN_V_DIM
        oT = (accT[0:nv, c1] / accT[nv:nv + 1, c1] - lam * (accT[0:nv, c2] / accT[nv:nv + 1, c2]))
        oT = oT * lax.rsqrt(jnp.mean(oT * oT, axis=0, keepdims=True) + EPS) * g_ref[...] * (1.0 - lam_init)
        o_ref[:, r * LANES:(r + 1) * LANES] = oT.T.astype(o_ref.dtype)


def _prompt_attention(slopes, qT, kb, vT, lamp, g_col, *, batch, T, tq, lam_init, unroll):
    nq = T // tq
    cols = 2 * ATTN_REP * tq
    return pl.pallas_call(
        functools.partial(_pattn_body, tq=tq, lam_init=lam_init, unroll=unroll),
        grid=(batch, ATTN_KV_HEADS, nq),
        in_specs=[pl.BlockSpec(memory_space=pltpu.SMEM),
                  pl.BlockSpec((ATTN_REP * LANES, tq), lambda b, h, i: (h, b * nq + i)),
                  pl.BlockSpec((T, LANES), lambda b, h, i: (b, h)),
                  pl.BlockSpec((nq, None, VT_ROWS, tq), lambda b, h, i: (b, h, 0, 0)),
                  _const_spec((4, ATTN_HEAD_DIM)), _const_spec((ATTN_V_DIM, 1))],
        out_specs=pl.BlockSpec((tq, ATTN_REP * LANES), lambda b, h, i: (b * nq + i, h)),
        out_shape=jax.ShapeDtypeStruct((batch * T, D_ATTN), bf16),
        scratch_shapes=[pltpu.VMEM((LANES, cols), bf16), pltpu.VMEM((ATTN_REP, tq, tq), f32),
                        pltpu.VMEM((tq, cols), f32), pltpu.VMEM((tq, cols), f32),
                        pltpu.VMEM((tq, cols), bf16), pltpu.VMEM((tq, cols), bf16),
                        pltpu.VMEM((1, cols), f32), pltpu.VMEM((1, cols), f32), pltpu.VMEM((1, cols), f32),
                        pltpu.VMEM((1, cols), f32), pltpu.VMEM((1, cols), f32),
                        pltpu.VMEM((VT_ROWS, cols), f32)],
        compiler_params=_cparams("parallel", "parallel", "arbitrary"),
        name="prompt_attention",
    )(slopes, qT, kb, vT, lamp, g_col)


def _sattn_body(pt_ref, slopes_ref, q_ref, kn_ref, vn_ref, *rest, n_pages, page, ts, lam_init):
    k_pages = rest[:n_pages]
    v_pages = rest[n_pages:2 * n_pages]
    lamp_ref, g_ref, o_ref = rest[2 * n_pages:]
    del pt_ref
    past = n_pages * page
    R = SAMPLE_ROWS
    first = R - ts
    width = past + page

    row = lax.broadcasted_iota(jnp.int32, (R, width), 0)
    col = lax.broadcasted_iota(jnp.int32, (R, width), 1)
    is_new = col >= past
    kpos = col - jnp.where(is_new, first, 0)
    dist = (past + row - first - kpos).astype(f32)
    ok = jnp.logical_and(dist >= 0.0, jnp.logical_or(jnp.logical_not(is_new), col >= past + first))
    lane = lax.broadcasted_iota(jnp.int32, (R, LANES), 1)
    lam = _lambda(lamp_ref, lam_init)
    q = q_ref[...].astype(f32)
    zpad = jnp.zeros((page - R, LANES), bf16)

    heads = range(ATTN_KV_HEADS)
    scores = []
    for h in heads:
        groups = []
        for r in range(ATTN_REP):
            c0 = (h * ATTN_REP + r) * LANES
            groups.append(_mask_maps(q[:, c0:c0 + LANES], lane))
        q32 = jnp.concatenate([groups[0][0], groups[1][0], groups[0][1], groups[1][1]], axis=0).astype(bf16)
        hs = slice(h * LANES, (h + 1) * LANES)
        head_rows = pl.ds(h, page, stride=ATTN_KV_HEADS)
        kcat = jnp.concatenate([kp[head_rows, :].astype(bf16) for kp in k_pages]
                               + [kn_ref[:, hs].astype(bf16), zpad], axis=0)
        scores.append(_dot_nt(q32, kcat))
    probs = []
    sums = []
    for h in heads:
        ps = []
        ls = []
        for g in range(2 * ATTN_REP):
            slope = slopes_ref[h * ATTN_REP + g % ATTN_REP]
            sg = jnp.where(ok, scores[h][g * R:(g + 1) * R, :] - slope * dist, NEG)
            p = jnp.exp(sg - jnp.max(sg, axis=-1, keepdims=True))
            ls.append(jnp.sum(p, axis=-1, keepdims=True))
            ps.append(p.astype(bf16))
        probs.append(jnp.concatenate(ps, axis=0))
        sums.append(ls)
    for h in heads:
        hs = slice(h * LANES, (h + 1) * LANES)
        head_rows = pl.ds(h, page, stride=ATTN_KV_HEADS)
        vcat = jnp.concatenate([vp[head_rows, :].astype(bf16) for vp in v_pages]
                               + [vn_ref[:, hs].astype(bf16), zpad], axis=0)
        acc = _dot(probs[h], vcat)
        ls = sums[h]
        for r in range(ATTN_REP):
            g1, g2 = r, ATTN_REP + r
            o = acc[g1 * R:(g1 + 1) * R, :] / ls[g1] - lam * (acc[g2 * R:(g2 + 1) * R, :] / ls[g2])
            o = _rms(o, g_ref[...]) * (1.0 - lam_init)
            c0 = (h * ATTN_REP + r) * LANES
            o_ref[:, c0:c0 + LANES] = o.astype(o_ref.dtype)


def _sample_attention(page_table, slopes, q, kn, vn, cache_k, cache_v, lamp, g, *, layer, ts, lam_init):
    bs, n_pages = page_table.shape
    page = cache_k.shape[2] // ATTN_KV_HEADS
    R = SAMPLE_ROWS

    def page_spec(p):
        return pl.BlockSpec((None, None, page * ATTN_KV_HEADS, LANES), lambda b, pt: (layer, pt[b, p], 0, 0))

    seq = lambda n: pl.BlockSpec((R, n), lambda b, pt: (b, 0))
    const = lambda shape: pl.BlockSpec(shape, lambda b, pt: (0,) * len(shape), pipeline_mode=pl.Buffered(1))
    grid_spec = pltpu.PrefetchScalarGridSpec(
        num_scalar_prefetch=1,
        grid=(bs,),
        in_specs=[pl.BlockSpec(memory_space=pltpu.SMEM), seq(Q_DIM), seq(K_DIM), seq(V_DIM)]
        + [page_spec(p) for p in range(n_pages)] * 2
        + [const((4, ATTN_HEAD_DIM)), const((1, ATTN_V_DIM))],
        out_specs=seq(D_ATTN),
    )
    return pl.pallas_call(
        functools.partial(_sattn_body, n_pages=n_pages, page=page, ts=ts, lam_init=lam_init),
        grid_spec=grid_spec,
        out_shape=jax.ShapeDtypeStruct((bs * R, D_ATTN), bf16),
        compiler_params=_cparams("parallel"),
        name="sample_attention",
    )(page_table, slopes, q, kn, vn, *([cache_k] * n_pages), *([cache_v] * n_pages), lamp, g)


def _alibi_slopes():
    return jnp.asarray(np.array([2.0 ** (-8.0 * (i + 1) / ATTN_HEADS) for i in range(ATTN_HEADS)], np.float32))


def _expand_matrix():
    e = np.zeros((LANES, D_SSD), np.float32)
    for h in range(SSD_HEADS):
        e[h, h * SSD_HEAD_DIM:(h + 1) * SSD_HEAD_DIM] = 1.0
    return jnp.asarray(np.tile(e, (N_PIECES, 1)), bf16)


def _pad_lanes(v):
    return jnp.pad(v, (0, LANES - v.shape[0]))[None, :]


def kernel(x_prompt, x_sample, cache_k, cache_v, state_ssm, state_conv, page_table, norm_mix, w_in, conv_w, conv_b, dt_bias, A_log, D_skip, ssd_norm, lambda_q1, lambda_k1, lambda_q2, lambda_k2, attn_subln, w_out, norm_mlp, w_up, w_down, norm_final):
    depth = w_in.shape[0]
    bp, T, _ = x_prompt.shape
    bs, ts, _ = x_sample.shape
    R = SAMPLE_ROWS
    assert T % SSD_CHUNK == 0 and N_PRE <= ts <= R - N_PRE and ts % SSD_CHUNK != 0
    mp = bp * T
    hp = SSD_HEADS * SSD_HEAD_DIM

    slopes = _alibi_slopes()
    emat = _expand_matrix()
    gf = norm_final[None, :]

    xp = x_prompt.reshape(mp, D_MODEL)
    xs = jnp.pad(x_sample, ((0, 0), (R - ts, 0), (0, 0))).reshape(bs * R, D_MODEL)
    state4 = state_ssm.reshape(depth, bs, hp, SSD_STATE)

    o0 = D_SSD
    o1 = o0 + CONV_DIM
    o2 = o1 + SSD_HEADS
    o3 = o2 + Q_DIM
    o4 = o3 + K_DIM

    n_pool = cache_k.shape[1]
    page_rows = cache_k.shape[2] * ATTN_KV_HEADS
    cache_k2 = cache_k.reshape(depth, n_pool, page_rows, LANES)
    cache_v2 = cache_v.reshape(depth, n_pool, page_rows, LANES)
    tk = 256

    outs = {k: [] for k in ("hp", "cp", "ks", "vs", "cs")}
    kv_stacks = ()
    hs_stack = ()
    for l in range(depth):
        lam_init = 0.8 - 0.6 * math.exp(-0.3 * l)
        wl = w_in[l]
        w_parts = (wl[:, :o0].astype(bf16), wl[:, o0:o1].astype(bf16),
                   jnp.pad(wl[:, o1:o2], ((0, 0), (0, LANES - SSD_HEADS))).astype(bf16),
                   wl[:, o2:o3].astype(bf16), wl[:, o3:o4].astype(bf16), wl[:, o4:].astype(bf16))
        g_mix = norm_mix[l][None, :]
        ssd_w = (conv_w[l], conv_b[l][None, :], _pad_lanes(dt_bias[l]), _pad_lanes(A_log[l]),
                 jnp.repeat(D_skip[l], SSD_HEAD_DIM)[None, :], ssd_norm[l][None, :], emat)
        lamp = jnp.stack([lambda_q1[l], lambda_k1[l], lambda_q2[l], lambda_k2[l]])
        g_sub = attn_subln[l][None, :]
        woy = w_out[l, :D_SSD].astype(bf16)
        woo = w_out[l, D_SSD:].astype(bf16)
        mlp_w = (norm_mlp[l][None, :], w_up[l].astype(bf16), w_down[l].astype(bf16), gf)
        final = l == depth - 1

        z, xbc, dt, qT, kb, vT, k_st, v_st = _in_proj_prompt(xp, g_mix, w_parts, kv_stacks, layer=l, depth=depth,
                                                             tm=512, tk=tk)
        kv_stacks = (k_st, v_st)
        y, h_new = _ssd_prompt(z, xbc, dt, ssd_w, batch=bp, L=SSD_CHUNK, n_chunks=T // SSD_CHUNK)
        o = _prompt_attention(slopes, qT, kb, vT, lamp, g_sub.reshape(ATTN_V_DIM, 1), batch=bp, T=T, tq=tk,
                              lam_init=lam_init, unroll=4)
        xp = _out_mlp(y, o, xp, woy, woo, *mlp_w, 512, final)
        outs["hp"].append(h_new.reshape(bp, SSD_HEADS, SSD_HEAD_DIM, SSD_STATE))
        outs["cp"].append(xbc.reshape(bp, T, CONV_DIM)[:, T - N_PRE:, :])

        z, xbc, dt, q, k, v = _in_proj_sample(xs, g_mix, w_parts, 512)
        y, hs_new = _ssd_sample(z, xbc, dt, state_conv, state4, ssd_w, hs_stack, layer=l, depth=depth, batch=bs,
                                ts=ts, n_seq=16)
        hs_stack = (hs_new,)
        o = _sample_attention(page_table, slopes, q, k, v, cache_k2, cache_v2, lamp, g_sub,
                              layer=l, ts=ts, lam_init=lam_init)
        xs = _out_mlp(y, o, xs, woy, woo, *mlp_w, 512, final)
        outs["ks"].append(k.reshape(bs, R, ATTN_KV_HEADS, 2 * ATTN_HEAD_DIM)[:, R - ts:])
        outs["vs"].append(v.reshape(bs, R, ATTN_KV_HEADS, ATTN_V_DIM)[:, R - ts:])
        outs["cs"].append(xbc.reshape(bs, R, CONV_DIM)[:, R - N_PRE:])

    y_prompt = xp.reshape(bp, T, D_MODEL)
    y_sample = xs.reshape(bs, R, D_MODEL)[:, R - ts:]
    new_k_prompt = kv_stacks[0].reshape(depth, bp, T, ATTN_KV_HEADS, 2 * ATTN_HEAD_DIM)
    new_v_prompt = kv_stacks[1].reshape(depth, bp, T, ATTN_KV_HEADS, ATTN_V_DIM)
    new_ssm_sample = hs_stack[0].reshape(depth, bs, SSD_HEADS, SSD_HEAD_DIM, SSD_STATE)
    st = lambda key: jnp.stack(outs[key])
    return (y_prompt, y_sample, new_k_prompt, new_v_prompt, st("hp"), st("cp"), st("ks"), st("vs"),
            new_ssm_sample, st("cs"))
```

```python
import functools
import math

import numpy as np
import jax
import jax.numpy as jnp
from jax import lax
from jax.experimental import pallas as pl
from jax.experimental.pallas import tpu as pltpu

f32 = jnp.float32
bf16 = jnp.bfloat16

D_MODEL = 1024
D_SSD = 1024
D_ATTN = 1024
SSD_HEAD_DIM = 64
SSD_HEADS = 16
SSD_GROUPS = 2
SSD_STATE = 128
SSD_CONV = 4
SSD_CHUNK = 128
CONV_DIM = D_SSD + 2 * SSD_GROUPS * SSD_STATE
ATTN_HEAD_DIM = 64
ATTN_V_DIM = 128
ATTN_HEADS = 8
ATTN_KV_HEADS = 4
ATTN_REP = 2
Q_DIM = ATTN_HEADS * 2 * ATTN_HEAD_DIM
K_DIM = ATTN_KV_HEADS * 2 * ATTN_HEAD_DIM
V_DIM = ATTN_KV_HEADS * ATTN_V_DIM
D_FF = 4 * D_MODEL
EPS = 1e-6
NEG = -1e30
LOG2E = math.log2(math.e)

LANES = 128
SUBLANES = 8
VMEM_LIMIT_BYTES = 56 * 1024 * 1024

SAMPLE_ROWS = SUBLANES
GROUP_W = D_SSD // SSD_GROUPS
HEADS_PER_GROUP = SSD_HEADS // SSD_GROUPS
N_PRE = SSD_CONV - 1
VT_ROWS = ATTN_V_DIM + 16
PROMPT_TQ = 512
PROMPT_TK = 512


def _row_tile(m, want):
    t = min(want, m)
    while m % t:
        t -= SUBLANES
    return t


def _cparams(*sem):
    return pltpu.CompilerParams(dimension_semantics=sem, vmem_limit_bytes=VMEM_LIMIT_BYTES)


def _const_spec(shape):
    nd = len(shape)
    return pl.BlockSpec(shape, lambda *_: (0,) * nd, pipeline_mode=pl.Buffered(1))


def _rms(x, g):
    return x * lax.rsqrt(jnp.mean(x * x, axis=-1, keepdims=True) + EPS) * g


def _silu(x):
    return x * jax.nn.sigmoid(x)


def _dot(a, b):
    return jnp.dot(a, b, preferred_element_type=f32)


def _dot_nt(a, b):
    return lax.dot_general(a, b, (((1,), (1,)), ((), ())), preferred_element_type=f32)


IN_WIDTHS = (D_SSD, CONV_DIM, LANES, Q_DIM, K_DIM, V_DIM)


def _inproj_common(x_ref, g_ref, wz, wxbc, wdt, wq, z_o, xbc_o, dt_o):
    xb = _rms(x_ref[...], g_ref[...]).astype(bf16)
    z_o[...] = _dot(xb, wz[...])
    xbc_o[...] = _dot(xb, wxbc[...])
    dt_o[...] = _dot(xb, wdt[...])
    return xb, _dot(xb, wq[...]) * (ATTN_HEAD_DIM ** -0.5)


def _inproj_sample_body(x_ref, g_ref, wz, wxbc, wdt, wq, wk, wv, z_o, xbc_o, dt_o, q_o, k_o, v_o):
    xb, q = _inproj_common(x_ref, g_ref, wz, wxbc, wdt, wq, z_o, xbc_o, dt_o)
    q_o[...] = q.astype(bf16)
    k_o[...] = _dot(xb, wk[...])
    v_o[...] = _dot(xb, wv[...])


def _inproj_prompt_body(x_ref, g_ref, wz, wxbc, wdt, wq, wk, wv, *rest, tk, n_prev):
    z_o, xbc_o, dt_o, qT_o, kb_o, vT_o, kst_o, vst_o = rest[n_prev:]
    xb, q = _inproj_common(x_ref, g_ref, wz, wxbc, wdt, wq, z_o, xbc_o, dt_o)
    tm = q.shape[0]
    qT_o[...] = (q * LOG2E).T.astype(bf16)
    k = _dot(xb, wk[...])
    v = _dot(xb, wv[...])
    kb_o[...] = k.astype(bf16)
    vT = v.T.astype(bf16)
    ones = jnp.ones((VT_ROWS - ATTN_V_DIM, tk), bf16)
    for i in range(tm // tk):
        for h in range(ATTN_KV_HEADS):
            vT_o[i, h, 0:ATTN_V_DIM, :] = vT[h * ATTN_V_DIM:(h + 1) * ATTN_V_DIM, i * tk:(i + 1) * tk]
            vT_o[i, h, ATTN_V_DIM:VT_ROWS, :] = ones
    for h in range(ATTN_KV_HEADS):
        hs = slice(h * LANES, (h + 1) * LANES)
        kst_o[pl.ds(h, tm, stride=ATTN_KV_HEADS), :] = k[:, hs]
        vst_o[pl.ds(h, tm, stride=ATTN_KV_HEADS), :] = v[:, hs]


def _in_proj_sample(x, g, w, tm):
    m = x.shape[0]
    tm = _row_tile(m, tm)
    row = lambda n: pl.BlockSpec((tm, n), lambda i: (i, 0))
    out_dt = (f32, f32, f32, bf16, f32, f32)
    return pl.pallas_call(
        _inproj_sample_body,
        grid=(m // tm,),
        in_specs=[row(D_MODEL), _const_spec((1, D_MODEL))] + [_const_spec((D_MODEL, n)) for n in IN_WIDTHS],
        out_specs=[row(n) for n in IN_WIDTHS],
        out_shape=[jax.ShapeDtypeStruct((m, n), d) for n, d in zip(IN_WIDTHS, out_dt)],
        compiler_params=_cparams("parallel"),
        name="in_proj_sample",
    )(x, g, *w)


def _in_proj_prompt(x, g, w, prev_stacks, *, layer, depth, tm, tk):
    m = x.shape[0]
    tm = _row_tile(m, tm)
    assert tm % tk == 0
    row = lambda n: pl.BlockSpec((tm, n), lambda i: (i, 0))
    n_prev = len(prev_stacks)
    stack_spec = pl.BlockSpec((None, tm * ATTN_KV_HEADS, LANES), lambda i: (layer, i, 0))
    stack_shape = jax.ShapeDtypeStruct((depth, m * ATTN_KV_HEADS, LANES), f32)
    n_in = 2 + len(IN_WIDTHS)
    return pl.pallas_call(
        functools.partial(_inproj_prompt_body, tk=tk, n_prev=n_prev),
        grid=(m // tm,),
        in_specs=[row(D_MODEL), _const_spec((1, D_MODEL))] + [_const_spec((D_MODEL, n)) for n in IN_WIDTHS]
        + [pl.BlockSpec(memory_space=pl.ANY)] * n_prev,
        out_specs=[row(D_SSD), row(CONV_DIM), row(LANES),
                   pl.BlockSpec((Q_DIM, tm), lambda i: (0, i)), row(K_DIM),
                   pl.BlockSpec((tm // tk, ATTN_KV_HEADS, VT_ROWS, tk), lambda i: (i, 0, 0, 0)),
                   stack_spec, stack_spec],
        out_shape=[jax.ShapeDtypeStruct((m, D_SSD), f32), jax.ShapeDtypeStruct((m, CONV_DIM), f32),
                   jax.ShapeDtypeStruct((m, LANES), f32), jax.ShapeDtypeStruct((Q_DIM, m), bf16),
                   jax.ShapeDtypeStruct((m, K_DIM), bf16),
                   jax.ShapeDtypeStruct((m // tk, ATTN_KV_HEADS, VT_ROWS, tk), bf16),
                   stack_shape, stack_shape],
        input_output_aliases={n_in + j: 6 + j for j in range(n_prev)},
        compiler_params=_cparams("parallel"),
        name="in_proj_prompt",
    )(x, g, *w, *prev_stacks)


def _outmlp_body(y_ref, o_ref, x_ref, woy, woo, g_ref, wup, wdn, gf_ref, out_ref, *, final, ff_chunk):
    x1 = x_ref[...] + _dot(y_ref[...], woy[...]) + _dot(o_ref[...], woo[...])
    h = _rms(x1, g_ref[...]).astype(bf16)
    mlp = None
    for j in range(D_FF // ff_chunk):
        u = _dot(h, wup[:, j * ff_chunk:(j + 1) * ff_chunk])
        u = jnp.square(jnp.maximum(u, 0.0)).astype(bf16)
        d = _dot(u, wdn[j * ff_chunk:(j + 1) * ff_chunk, :])
        mlp = d if mlp is None else mlp + d
    x2 = x1 + mlp
    out_ref[...] = _rms(x2, gf_ref[...]) if final else x2


def _out_mlp(y, o, x, woy, woo, g, wup, wdn, gf, tm, final):
    m = x.shape[0]
    tm = _row_tile(m, tm)
    row = lambda n: pl.BlockSpec((tm, n), lambda i: (i, 0))
    return pl.pallas_call(
        functools.partial(_outmlp_body, final=final, ff_chunk=1024),
        grid=(m // tm,),
        in_specs=[row(D_SSD), row(D_ATTN), row(D_MODEL),
                  _const_spec((D_SSD, D_MODEL)), _const_spec((D_ATTN, D_MODEL)), _const_spec((1, D_MODEL)),
                  _const_spec((D_MODEL, D_FF)), _const_spec((D_FF, D_MODEL)), _const_spec((1, D_MODEL))],
        out_specs=row(D_MODEL),
        out_shape=jax.ShapeDtypeStruct((m, D_MODEL), f32),
        compiler_params=_cparams("parallel"),
        name="out_mlp",
    )(y, o, x, woy, woo, g, wup, wdn, gf)


N_PIECES = 3


def _split3(x):
    p1 = x.astype(bf16)
    r1 = x - p1.astype(f32)
    p2 = r1.astype(bf16)
    p3 = (r1 - p2.astype(f32)).astype(bf16)
    return jnp.concatenate([p1, p2, p3], axis=1)


def _expand(v, emat3):
    return _dot(_split3(v), emat3)


def _eye_bf16(reps):
    ri = lax.broadcasted_iota(jnp.int32, (LANES, reps * LANES), 0)
    ci = jnp.bitwise_and(lax.broadcasted_iota(jnp.int32, (LANES, reps * LANES), 1), LANES - 1)
    return jnp.where(ri == ci, 1.0, 0.0).astype(bf16)


def _transpose_f32(x, eye3):
    return _dot_nt(eye3, _split3(x))


def _conv_taps(cbuf, w, bias, L):
    conv = bias
    for i in range(SSD_CONV):
        off = SUBLANES - N_PRE + i
        conv = conv + w[i:i + 1, :] * cbuf[off:off + L, :]
    return conv


def _intra_chunk(cm, bm, xdt, a_cs, a_cs_t, mask):
    L = cm.shape[0]
    lane = lax.broadcasted_iota(jnp.int32, (L, LANES), 1)
    halves = (lane < SSD_HEAD_DIM, lane >= SSD_HEAD_DIM)
    out = []
    for g in range(SSD_GROUPS):
        cg = cm[:, g * SSD_STATE:(g + 1) * SSD_STATE].astype(bf16)
        bg = bm[:, g * SSD_STATE:(g + 1) * SSD_STATE].astype(bf16)
        cbm = _dot_nt(cg, bg)
        blocks = []
        for jp in range(HEADS_PER_GROUP // 2):
            pair = g * (HEADS_PER_GROUP // 2) + jp
            xp = xdt[:, pair * LANES:(pair + 1) * LANES]
            acc = None
            for k in range(2):
                h = 2 * pair + k
                seg = a_cs[:, h:h + 1] - a_cs_t[h:h + 1, :]
                dec = jnp.where(mask, jnp.exp(seg), 0.0)
                mh = (cbm * dec).astype(bf16)
                part = _dot(mh, jnp.where(halves[k], xp, 0.0).astype(bf16))
                acc = part if acc is None else acc + part
            blocks.append(acc)
        out.append(blocks)
    return out


def _ssd_prompt_body(z_ref, xbc_ref, dt_ref, cw_ref, cb_ref, dtb_ref, alog_ref, dsk_ref, g_ref, e_ref,
                     y_ref, hout_ref, cbuf, hT, *, L, n_chunks):
    c = pl.program_id(1)

    @pl.when(c == 0)
    def _():
        cbuf[0:SUBLANES, :] = jnp.zeros((SUBLANES, CONV_DIM), f32)
        hT[...] = jnp.zeros_like(hT)

    xraw = xbc_ref[...]
    cbuf[SUBLANES:SUBLANES + L, :] = xraw
    act = _silu(_conv_taps(cbuf, cw_ref[...], cb_ref[...], L))
    cbuf[0:SUBLANES, :] = xraw[L - SUBLANES:L, :]
    xs = act[:, :D_SSD]
    bm = act[:, D_SSD:D_SSD + SSD_GROUPS * SSD_STATE]
    cm = act[:, D_SSD + SSD_GROUPS * SSD_STATE:]

    rows = lax.broadcasted_iota(jnp.int32, (L, LANES), 0)
    dt = jax.nn.softplus(dt_ref[...] + dtb_ref[...])
    a = dt * (-jnp.exp(alog_ref[...]))
    a_cs = a
    s = 1
    while s < L:
        a_cs = a_cs + jnp.where(rows >= s, pltpu.roll(a_cs, s, 0), 0.0)
        s *= 2
    ea = jnp.exp(a_cs)
    te = jnp.exp(a_cs[L - 1:L, :] - a_cs)

    emat = e_ref[...]
    ea_e = _expand(ea, emat)
    xdt = xs * _expand(dt, emat)
    wst = (xdt * _expand(te, emat)).astype(bf16)

    eye = _eye_bf16(1)
    a_cs_t = _transpose_f32(a_cs, _eye_bf16(N_PIECES))
    causal = lax.broadcasted_iota(jnp.int32, (L, L), 0) >= lax.broadcasted_iota(jnp.int32, (L, L), 1)
    intra = _intra_chunk(cm, bm, xdt, a_cs, a_cs_t, causal)

    y_parts = []
    for g in range(SSD_GROUPS):
        gs = slice(g * GROUP_W, (g + 1) * GROUP_W)
        cg = cm[:, g * SSD_STATE:(g + 1) * SSD_STATE].astype(bf16)
        bg = bm[:, g * SSD_STATE:(g + 1) * SSD_STATE].astype(bf16)
        h_g = hT[:, gs]
        y_inter = _dot(cg, h_g.astype(bf16)) * ea_e[:, gs]
        bg_t = _dot_nt(eye, bg).astype(bf16)
        hT[:, gs] = h_g * ea_e[L - 1:L, gs] + _dot(bg_t, wst[:, gs])
        y_parts.append(jnp.concatenate(intra[g], axis=1) + y_inter)
    y = jnp.concatenate(y_parts, axis=1) + xs * dsk_ref[...]
    y = y * _silu(z_ref[...])
    y_ref[...] = _rms(y, g_ref[...]).astype(y_ref.dtype)

    @pl.when(c == n_chunks - 1)
    def _():
        hout_ref[...] = hT[...].T


def _ssd_weight_specs():
    return [_const_spec((SSD_CONV, CONV_DIM)), _const_spec((1, CONV_DIM)), _const_spec((1, LANES)),
            _const_spec((1, LANES)), _const_spec((1, D_SSD)), _const_spec((1, D_SSD)),
            _const_spec((N_PIECES * LANES, D_SSD))]


def _ssd_prompt(z, xbc, dt, ssd_w, *, batch, L, n_chunks):
    hp = SSD_HEADS * SSD_HEAD_DIM
    rowspec = lambda n: pl.BlockSpec((L, n), lambda b, c: (b * n_chunks + c, 0))
    return pl.pallas_call(
        functools.partial(_ssd_prompt_body, L=L, n_chunks=n_chunks),
        grid=(batch, n_chunks),
        in_specs=[rowspec(D_SSD), rowspec(CONV_DIM), rowspec(LANES)] + _ssd_weight_specs(),
        out_specs=[rowspec(D_SSD), pl.BlockSpec((None, hp, SSD_STATE), lambda b, c: (b, 0, 0))],
        out_shape=[jax.ShapeDtypeStruct((batch * n_chunks * L, D_SSD), bf16),
                   jax.ShapeDtypeStruct((batch, hp, SSD_STATE), f32)],
        scratch_shapes=[pltpu.VMEM((SUBLANES + L, CONV_DIM), f32), pltpu.VMEM((SSD_STATE, hp), f32)],
        compiler_params=_cparams("parallel", "arbitrary"),
        name="ssd_prompt",
    )(z, xbc, dt, *ssd_w)


def _ssd_sample_body(z_ref, xbc_ref, dt_ref, pre_ref, h0_ref, cw_ref, cb_ref, dtb_ref, alog_ref, dsk_ref, g_ref,
                     e_ref, *rest, n_seq, ts):
    y_ref, hout_ref, cbuf, yint = rest[-4:]
    R = SAMPLE_ROWS
    L = n_seq * R
    first = R - ts

    cbuf[0:SUBLANES, :] = jnp.zeros((SUBLANES, CONV_DIM), f32)
    cbuf[SUBLANES:SUBLANES + L, :] = xbc_ref[...]
    for s in range(n_seq):
        r0 = SUBLANES + s * R + first - N_PRE
        cbuf[r0:r0 + N_PRE, :] = pre_ref[s]
    act = _silu(_conv_taps(cbuf, cw_ref[...], cb_ref[...], L))
    xs = act[:, :D_SSD]
    bm = act[:, D_SSD:D_SSD + SSD_GROUPS * SSD_STATE]
    cm = act[:, D_SSD + SSD_GROUPS * SSD_STATE:]

    slot = jnp.bitwise_and(lax.broadcasted_iota(jnp.int32, (L, LANES), 0), R - 1)
    dt = jnp.where(slot >= first, jax.nn.softplus(dt_ref[...] + dtb_ref[...]), 0.0)
    a = dt * (-jnp.exp(alog_ref[...]))
    a_cs = a
    a_sf = a
    s = 1
    while s < R:
        a_cs = a_cs + jnp.where(slot >= s, pltpu.roll(a_cs, s, 0), 0.0)
        a_sf = a_sf + jnp.where(slot < R - s, pltpu.roll(a_sf, L - s, 0), 0.0)
        s *= 2
    ea = jnp.exp(a_cs)
    te = jnp.exp(a_sf - a)
    cd = jnp.exp(a_cs + a_sf - a)

    emat = e_ref[...]
    ea_e = _expand(ea, emat)
    xdt = xs * _expand(dt, emat)
    w_t = (xdt * _expand(te, emat)).T
    cd_t = _expand(cd, emat).T

    eye = _eye_bf16(1)
    a_cs_t = _transpose_f32(a_cs, _eye_bf16(N_PIECES))
    li = lax.broadcasted_iota(jnp.int32, (L, L), 0)
    si = lax.broadcasted_iota(jnp.int32, (L, L), 1)
    mask = jnp.logical_and(li >= si, jnp.bitwise_and(li, -R) == jnp.bitwise_and(si, -R))
    intra = _intra_chunk(cm, bm, xdt, a_cs, a_cs_t, mask)

    col = lax.broadcasted_iota(jnp.int32, (GROUP_W, L), 1)
    cmb = cm.astype(bf16)
    bmb = bm.astype(bf16)
    for s in range(n_seq):
        rs = slice(s * R, (s + 1) * R)
        mine = jnp.logical_and(col >= s * R, col < (s + 1) * R)
        for g in range(SSD_GROUPS):
            gs = slice(g * GROUP_W, (g + 1) * GROUP_W)
            ns = slice(g * SSD_STATE, (g + 1) * SSD_STATE)
            h0 = h0_ref[s, gs, :]
            yint[rs, gs] = _dot_nt(cmb[rs, ns], h0.astype(bf16))
            upd = _dot(jnp.where(mine, w_t[gs, :], 0.0).astype(bf16), bmb[:, ns])
            hout_ref[s, gs, :] = h0 * cd_t[gs, s * R:s * R + 1] + upd

    y = jnp.concatenate(intra[0] + intra[1], axis=1) + yint[...] * ea_e + xs * dsk_ref[...]
    y = y * _silu(z_ref[...])
    y_ref[...] = _rms(y, g_ref[...]).astype(y_ref.dtype)


def _ssd_sample(z, xbc, dt, pre, h0, ssd_w, prev_stack, *, layer, depth, batch, ts, n_seq):
    hp = SSD_HEADS * SSD_HEAD_DIM
    L = n_seq * SAMPLE_ROWS
    assert batch % n_seq == 0
    rowspec = lambda n: pl.BlockSpec((L, n), lambda i: (i, 0))
    seqspec = lambda a, b: pl.BlockSpec((None, n_seq, a, b), lambda i: (layer, i, 0, 0))
    n_prev = len(prev_stack)
    n_in = 5 + len(_ssd_weight_specs())
    return pl.pallas_call(
        functools.partial(_ssd_sample_body, n_seq=n_seq, ts=ts),
        grid=(batch // n_seq,),
        in_specs=[rowspec(D_SSD), rowspec(CONV_DIM), rowspec(LANES), seqspec(N_PRE, CONV_DIM),
                  seqspec(hp, SSD_STATE)] + _ssd_weight_specs() + [pl.BlockSpec(memory_space=pl.ANY)] * n_prev,
        out_specs=[rowspec(D_SSD), seqspec(hp, SSD_STATE)],
        out_shape=[jax.ShapeDtypeStruct((batch * SAMPLE_ROWS, D_SSD), bf16),
                   jax.ShapeDtypeStruct((depth, batch, hp, SSD_STATE), f32)],
        input_output_aliases={n_in + j: 1 + j for j in range(n_prev)},
        scratch_shapes=[pltpu.VMEM((SUBLANES + L, CONV_DIM), f32), pltpu.VMEM((L, D_SSD), f32)],
        compiler_params=_cparams("parallel"),
        name="ssd_sample",
    )(z, xbc, dt, pre, h0, *ssd_w, *prev_stack)


def _lambda(lamp_ref, lam_init):
    lp = lamp_ref[...]
    t1 = jnp.sum(lp[0:1, :] * lp[1:2, :], axis=-1, keepdims=True)
    t2 = jnp.sum(lp[2:3, :] * lp[3:4, :], axis=-1, keepdims=True)
    return jnp.exp(t1) - jnp.exp(t2) + lam_init


def _mask_maps(qr, lane):
    zero = jnp.zeros_like(qr)
    return jnp.where(lane < ATTN_HEAD_DIM, qr, zero), jnp.where(lane >= ATTN_HEAD_DIM, qr, zero)


def _pattn_body(slopes_ref, qT_ref, k_ref, vT_ref, lamp_ref, g_ref, o_ref,
                q4T, sb_ref, s0, s1, p0, p1, al0, al1, mx0, mx1, m_ref, accT, *, tq, tk, lam_init, unroll):
    kv = pl.program_id(1)
    qi = pl.program_id(2)
    n_grp = 2 * ATTN_REP
    hd = ATTN_HEAD_DIM
    n_diag = tq // tk
    n_full = qi * n_diag

    zero = jnp.zeros((hd, tq), bf16)
    for r in range(ATTN_REP):
        c1 = slice(r * tq, (r + 1) * tq)
        c2 = slice((ATTN_REP + r) * tq, (ATTN_REP + r + 1) * tq)
        q4T[0:hd, c1] = qT_ref[r * LANES:r * LANES + hd, :]
        q4T[hd:2 * hd, c1] = zero
        q4T[0:hd, c2] = zero
        q4T[hd:2 * hd, c2] = qT_ref[r * LANES + hd:(r + 1) * LANES, :]
    m_ref[...] = jnp.full_like(m_ref, NEG)
    accT[...] = jnp.zeros_like(accT)
    s_buf, p_buf, al_buf, mx_buf = (s0, s1), (p0, p1), (al0, al1), (mx0, mx1)
    al1[...] = jnp.ones_like(al1)
    p1[...] = jnp.zeros_like(p1)

    rel = (lax.broadcasted_iota(jnp.int32, (tk, tq), 1) - lax.broadcasted_iota(jnp.int32, (tk, tq), 0)).astype(f32)
    slope = [slopes_ref[kv * ATTN_REP + r] * LOG2E for r in range(ATTN_REP)]

    @pl.when(qi == 0)
    def _():
        for r in range(ATTN_REP):
            sb_ref[r] = slope[r] * rel

    def scores(j, slot):
        start = pl.multiple_of(j * tk, tk)
        kblk = k_ref[pl.ds(start, tk), :]
        for g in range(n_grp):
            cs = slice(g * tq, (g + 1) * tq)
            s = _dot(kblk, q4T[:, cs]) - sb_ref[g % ATTN_REP]
            s_buf[slot][:, cs] = s
            mx_buf[slot][:, cs] = jnp.max(s, axis=0, keepdims=True)

    def softmax(j, slot, diag):
        blk_dist = (qi * tq - j * tk).astype(f32)
        for g in range(n_grp):
            cs = slice(g * tq, (g + 1) * tq)
            s = s_buf[slot][:, cs]
            if diag is None:
                mx = mx_buf[slot][:, cs]
            else:
                s = jnp.where(rel >= float(diag * tk), s, NEG)
                mx = jnp.max(s, axis=0, keepdims=True)
            off = slope[g % ATTN_REP] * blk_dist
            m_old = m_ref[:, cs]
            m_new = jnp.maximum(m_old, mx - off)
            m_ref[:, cs] = m_new
            al_buf[slot][:, cs] = jnp.exp2(m_old - m_new)
            p_buf[slot][:, cs] = jnp.exp2(s - (m_new + off)).astype(bf16)

    def values(j, slot):
        accT[...] = accT[...] * al_buf[slot][...] + _dot(vT_ref[j], p_buf[slot][...])

    def stage(j, slot, diag=None, more=True):
        if more:
            scores(j + 1, 1 - slot)
        softmax(j, slot, diag)
        values(jnp.maximum(j - 1, 0), 1 - slot)

    scores(0, 0)

    shift = unroll.bit_length() - 1

    def group_body(i, carry):
        for u in range(unroll):
            stage(unroll * i + u, u % 2)
        return carry

    lax.fori_loop(0, jnp.right_shift(n_full, shift), group_body, 0)
    rem = jnp.bitwise_and(n_full, unroll - 1)
    base = n_full - rem
    for u in range(unroll - 1):
        @pl.when(rem > u)
        def _():
            stage(base + u, u % 2)

    def tail(parity):
        for d in range(n_diag):
            stage(n_full + d, (parity + d) % 2, diag=d, more=d + 1 < n_diag)
        values(n_full + n_diag - 1, (parity + n_diag - 1) % 2)

    if n_diag % 2 == 0:
        tail(0)
    else:
        for parity in range(2):
            @pl.when(jnp.bitwise_and(n_full, 1) == parity)
            def _():
                tail(parity)

    lam = _lambda(lamp_ref, lam_init)
    for r in range(ATTN_REP):
        c1 = slice(r * tq, (r + 1) * tq)
        c2 = slice((ATTN_REP + r) * tq, (ATTN_REP + r + 1) * tq)
        nv = ATTN_V_DIM
        oT = (accT[0:nv, c1] / accT[nv:nv + 1, c1] - lam * (accT[0:nv, c2] / accT[nv:nv + 1, c2]))
        oT = oT * lax.rsqrt(jnp.mean(oT * oT, axis=0, keepdims=True) + EPS) * g_ref[...] * (1.0 - lam_init)
        o_ref[:, r * LANES:(r + 1) * LANES] = oT.T.astype(o_ref.dtype)


def _prompt_attention(slopes, qT, kb, vT, lamp, g_col, *, batch, T, tq, tk, lam_init, unroll):
    assert T % tq == 0 and tq % tk == 0
    nq = T // tq
    cols = 2 * ATTN_REP * tq
    row_stat = pltpu.VMEM((1, cols), f32)
    return pl.pallas_call(
        functools.partial(_pattn_body, tq=tq, tk=tk, lam_init=lam_init, unroll=unroll),
        grid=(batch, ATTN_KV_HEADS, nq),
        in_specs=[pl.BlockSpec(memory_space=pltpu.SMEM),
                  pl.BlockSpec((ATTN_REP * LANES, tq), lambda b, h, i: (h, b * nq + i)),
                  pl.BlockSpec((T, LANES), lambda b, h, i: (b, h)),
                  pl.BlockSpec((T // tk, None, VT_ROWS, tk), lambda b, h, i: (b, h, 0, 0)),
                  _const_spec((4, ATTN_HEAD_DIM)), _const_spec((ATTN_V_DIM, 1))],
        out_specs=pl.BlockSpec((tq, ATTN_REP * LANES), lambda b, h, i: (b * nq + i, h)),
        out_shape=jax.ShapeDtypeStruct((batch * T, D_ATTN), bf16),
        scratch_shapes=[pltpu.VMEM((LANES, cols), bf16), pltpu.VMEM((ATTN_REP, tk, tq), f32),
                        pltpu.VMEM((tk, cols), f32), pltpu.VMEM((tk, cols), f32),
                        pltpu.VMEM((tk, cols), bf16), pltpu.VMEM((tk, cols), bf16),
                        row_stat, row_stat, row_stat, row_stat, row_stat,
                        pltpu.VMEM((VT_ROWS, cols), f32)],
        compiler_params=_cparams("parallel", "parallel", "arbitrary"),
        name="prompt_attention",
    )(slopes, qT, kb, vT, lamp, g_col)


def _sattn_body(pt_ref, slopes_ref, q_ref, kn_ref, vn_ref, *rest, n_pages, page, ts, lam_init):
    k_pages = rest[:n_pages]
    v_pages = rest[n_pages:2 * n_pages]
    lamp_ref, g_ref, o_ref = rest[2 * n_pages:]
    del pt_ref
    past = n_pages * page
    R = SAMPLE_ROWS
    first = R - ts
    width = past + page

    row = lax.broadcasted_iota(jnp.int32, (R, width), 0)
    col = lax.broadcasted_iota(jnp.int32, (R, width), 1)
    is_new = col >= past
    kpos = col - jnp.where(is_new, first, 0)
    dist = (past + row - first - kpos).astype(f32)
    ok = jnp.logical_and(dist >= 0.0, jnp.logical_or(jnp.logical_not(is_new), col >= past + first))
    lane = lax.broadcasted_iota(jnp.int32, (R, LANES), 1)
    lam = _lambda(lamp_ref, lam_init)
    q = q_ref[...].astype(f32)
    zpad = jnp.zeros((page - R, LANES), bf16)

    heads = range(ATTN_KV_HEADS)
    scores = []
    for h in heads:
        groups = []
        for r in range(ATTN_REP):
            c0 = (h * ATTN_REP + r) * LANES
            groups.append(_mask_maps(q[:, c0:c0 + LANES], lane))
        q32 = jnp.concatenate([groups[0][0], groups[1][0], groups[0][1], groups[1][1]], axis=0).astype(bf16)
        hs = slice(h * LANES, (h + 1) * LANES)
        head_rows = pl.ds(h, page, stride=ATTN_KV_HEADS)
        kcat = jnp.concatenate([kp[head_rows, :].astype(bf16) for kp in k_pages]
                               + [kn_ref[:, hs].astype(bf16), zpad], axis=0)
        scores.append(_dot_nt(q32, kcat))
    probs = []
    sums = []
    for h in heads:
        ps = []
        ls = []
        for g in range(2 * ATTN_REP):
            slope = slopes_ref[h * ATTN_REP + g % ATTN_REP]
            sg = jnp.where(ok, scores[h][g * R:(g + 1) * R, :] - slope * dist, NEG)
            p = jnp.exp(sg - jnp.max(sg, axis=-1, keepdims=True))
            ls.append(jnp.sum(p, axis=-1, keepdims=True))
            ps.append(p.astype(bf16))
        probs.append(jnp.concatenate(ps, axis=0))
        sums.append(ls)
    for h in heads:
        hs = slice(h * LANES, (h + 1) * LANES)
        head_rows = pl.ds(h, page, stride=ATTN_KV_HEADS)
        vcat = jnp.concatenate([vp[head_rows, :].astype(bf16) for vp in v_pages]
                               + [vn_ref[:, hs].astype(bf16), zpad], axis=0)
        acc = _dot(probs[h], vcat)
        ls = sums[h]
        for r in range(ATTN_REP):
            g1, g2 = r, ATTN_REP + r
            o = acc[g1 * R:(g1 + 1) * R, :] / ls[g1] - lam * (acc[g2 * R:(g2 + 1) * R, :] / ls[g2])
            o = _rms(o, g_ref[...]) * (1.0 - lam_init)
            c0 = (h * ATTN_REP + r) * LANES
            o_ref[:, c0:c0 + LANES] = o.astype(o_ref.dtype)


def _sample_attention(page_table, slopes, q, kn, vn, cache_k, cache_v, lamp, g, *, layer, ts, lam_init):
    bs, n_pages = page_table.shape
    page = cache_k.shape[2] // ATTN_KV_HEADS
    R = SAMPLE_ROWS

    def page_spec(p):
        return pl.BlockSpec((None, None, page * ATTN_KV_HEADS, LANES), lambda b, pt: (layer, pt[b, p], 0, 0))

    seq = lambda n: pl.BlockSpec((R, n), lambda b, pt: (b, 0))
    const = lambda shape: pl.BlockSpec(shape, lambda b, pt: (0,) * len(shape), pipeline_mode=pl.Buffered(1))
    grid_spec = pltpu.PrefetchScalarGridSpec(
        num_scalar_prefetch=1,
        grid=(bs,),
        in_specs=[pl.BlockSpec(memory_space=pltpu.SMEM), seq(Q_DIM), seq(K_DIM), seq(V_DIM)]
        + [page_spec(p) for p in range(n_pages)] * 2
        + [const((4, ATTN_HEAD_DIM)), const((1, ATTN_V_DIM))],
        out_specs=seq(D_ATTN),
    )
    return pl.pallas_call(
        functools.partial(_sattn_body, n_pages=n_pages, page=page, ts=ts, lam_init=lam_init),
        grid_spec=grid_spec,
        out_shape=jax.ShapeDtypeStruct((bs * R, D_ATTN), bf16),
        compiler_params=_cparams("parallel"),
        name="sample_attention",
    )(page_table, slopes, q, kn, vn, *([cache_k] * n_pages), *([cache_v] * n_pages), lamp, g)


def _alibi_slopes():
    return jnp.asarray(np.array([2.0 ** (-8.0 * (i + 1) / ATTN_HEADS) for i in range(ATTN_HEADS)], np.float32))


def _expand_matrix():
    e = np.zeros((LANES, D_SSD), np.float32)
    for h in range(SSD_HEADS):
        e[h, h * SSD_HEAD_DIM:(h + 1) * SSD_HEAD_DIM] = 1.0
    return jnp.asarray(np.tile(e, (N_PIECES, 1)), bf16)


def _pad_lanes(v):
    return jnp.pad(v, (0, LANES - v.shape[0]))[None, :]


def kernel(x_prompt, x_sample, cache_k, cache_v, state_ssm, state_conv, page_table, norm_mix, w_in, conv_w, conv_b, dt_bias, A_log, D_skip, ssd_norm, lambda_q1, lambda_k1, lambda_q2, lambda_k2, attn_subln, w_out, norm_mlp, w_up, w_down, norm_final):
    depth = w_in.shape[0]
    bp, T, _ = x_prompt.shape
    bs, ts, _ = x_sample.shape
    R = SAMPLE_ROWS
    assert T % SSD_CHUNK == 0 and N_PRE <= ts <= R - N_PRE and ts % SSD_CHUNK != 0
    mp = bp * T
    hp = SSD_HEADS * SSD_HEAD_DIM

    slopes = _alibi_slopes()
    emat = _expand_matrix()
    gf = norm_final[None, :]

    xp = x_prompt.reshape(mp, D_MODEL)
    xs = jnp.pad(x_sample, ((0, 0), (R - ts, 0), (0, 0))).reshape(bs * R, D_MODEL)
    state4 = state_ssm.reshape(depth, bs, hp, SSD_STATE)

    o0 = D_SSD
    o1 = o0 + CONV_DIM
    o2 = o1 + SSD_HEADS
    o3 = o2 + Q_DIM
    o4 = o3 + K_DIM

    n_pool = cache_k.shape[1]
    page_rows = cache_k.shape[2] * ATTN_KV_HEADS
    cache_k2 = cache_k.reshape(depth, n_pool, page_rows, LANES)
    cache_v2 = cache_v.reshape(depth, n_pool, page_rows, LANES)

    outs = {k: [] for k in ("hp", "cp", "ks", "vs", "cs")}
    kv_stacks = ()
    hs_stack = ()
    for l in range(depth):
        lam_init = 0.8 - 0.6 * math.exp(-0.3 * l)
        wl = w_in[l]
        w_parts = (wl[:, :o0].astype(bf16), wl[:, o0:o1].astype(bf16),
                   jnp.pad(wl[:, o1:o2], ((0, 0), (0, LANES - SSD_HEADS))).astype(bf16),
                   wl[:, o2:o3].astype(bf16), wl[:, o3:o4].astype(bf16), wl[:, o4:].astype(bf16))
        g_mix = norm_mix[l][None, :]
        ssd_w = (conv_w[l], conv_b[l][None, :], _pad_lanes(dt_bias[l]), _pad_lanes(A_log[l]),
                 jnp.repeat(D_skip[l], SSD_HEAD_DIM)[None, :], ssd_norm[l][None, :], emat)
        lamp = jnp.stack([lambda_q1[l], lambda_k1[l], lambda_q2[l], lambda_k2[l]])
        g_sub = attn_subln[l][None, :]
        woy = w_out[l, :D_SSD].astype(bf16)
        woo = w_out[l, D_SSD:].astype(bf16)
        mlp_w = (norm_mlp[l][None, :], w_up[l].astype(bf16), w_down[l].astype(bf16), gf)
        final = l == depth - 1

        z, xbc, dt, qT, kb, vT, k_st, v_st = _in_proj_prompt(xp, g_mix, w_parts, kv_stacks, layer=l, depth=depth,
                                                             tm=512, tk=PROMPT_TK)
        kv_stacks = (k_st, v_st)
        y, h_new = _ssd_prompt(z, xbc, dt, ssd_w, batch=bp, L=SSD_CHUNK, n_chunks=T // SSD_CHUNK)
        o = _prompt_attention(slopes, qT, kb, vT, lamp, g_sub.reshape(ATTN_V_DIM, 1), batch=bp, T=T,
                              tq=min(PROMPT_TQ, T), tk=PROMPT_TK, lam_init=lam_init, unroll=2)
        xp = _out_mlp(y, o, xp, woy, woo, *mlp_w, 512, final)
        outs["hp"].append(h_new.reshape(bp, SSD_HEADS, SSD_HEAD_DIM, SSD_STATE))
        outs["cp"].append(xbc.reshape(bp, T, CONV_DIM)[:, T - N_PRE:, :])

        z, xbc, dt, q, k, v = _in_proj_sample(xs, g_mix, w_parts, 512)
        y, hs_new = _ssd_sample(z, xbc, dt, state_conv, state4, ssd_w, hs_stack, layer=l, depth=depth, batch=bs,
                                ts=ts, n_seq=16)
        hs_stack = (hs_new,)
        o = _sample_attention(page_table, slopes, q, k, v, cache_k2, cache_v2, lamp, g_sub,
                              layer=l, ts=ts, lam_init=lam_init)
        xs = _out_mlp(y, o, xs, woy, woo, *mlp_w, 512, final)
        outs["ks"].append(k.reshape(bs, R, ATTN_KV_HEADS, 2 * ATTN_HEAD_DIM)[:, R - ts:])
        outs["vs"].append(v.reshape(bs, R, ATTN_KV_HEADS, ATTN_V_DIM)[:, R - ts:])
        outs["cs"].append(xbc.reshape(bs, R, CONV_DIM)[:, R - N_PRE:])

    y_prompt = xp.reshape(bp, T, D_MODEL)
    y_sample = xs.reshape(bs, R, D_MODEL)[:, R - ts:]
    new_k_prompt = kv_stacks[0].reshape(depth, bp, T, ATTN_KV_HEADS, 2 * ATTN_HEAD_DIM)
    new_v_prompt = kv_stacks[1].reshape(depth, bp, T, ATTN_KV_HEADS, ATTN_V_DIM)
    new_ssm_sample = hs_stack[0].reshape(depth, bs, SSD_HEADS, SSD_HEAD_DIM, SSD_STATE)
    st = lambda key: jnp.stack(outs[key])
    return (y_prompt, y_sample, new_k_prompt, new_v_prompt, st("hp"), st("cp"), st("ks"), st("vs"),
            new_ssm_sample, st("cs"))
```

```python
import functools
import math

import numpy as np
import jax
import jax.numpy as jnp
from jax import lax
from jax.experimental import pallas as pl
from jax.experimental.pallas import tpu as pltpu

f32 = jnp.float32
bf16 = jnp.bfloat16

D_MODEL = 1024
D_SSD = 1024
D_ATTN = 1024
SSD_HEAD_DIM = 64
SSD_HEADS = 16
SSD_GROUPS = 2
SSD_STATE = 128
SSD_CONV = 4
SSD_CHUNK = 128
CONV_DIM = D_SSD + 2 * SSD_GROUPS * SSD_STATE
ATTN_HEAD_DIM = 64
ATTN_V_DIM = 128
ATTN_HEADS = 8
ATTN_KV_HEADS = 4
ATTN_REP = 2
Q_DIM = ATTN_HEADS * 2 * ATTN_HEAD_DIM
K_DIM = ATTN_KV_HEADS * 2 * ATTN_HEAD_DIM
V_DIM = ATTN_KV_HEADS * ATTN_V_DIM
D_FF = 4 * D_MODEL
EPS = 1e-6
NEG = -1e30
LOG2E = math.log2(math.e)

LANES = 128
SUBLANES = 8
VMEM_LIMIT_BYTES = 56 * 1024 * 1024

SAMPLE_ROWS = SUBLANES
GROUP_W = D_SSD // SSD_GROUPS
HEADS_PER_GROUP = SSD_HEADS // SSD_GROUPS
N_PRE = SSD_CONV - 1
VT_ROWS = ATTN_V_DIM + 16
PROMPT_TQ = 512
PROMPT_TK = 512


def _row_tile(m, want):
    t = min(want, m)
    while m % t:
        t -= SUBLANES
    return t


def _cparams(*sem):
    return pltpu.CompilerParams(dimension_semantics=sem, vmem_limit_bytes=VMEM_LIMIT_BYTES)


def _const_spec(shape):
    nd = len(shape)
    return pl.BlockSpec(shape, lambda *_: (0,) * nd, pipeline_mode=pl.Buffered(1))


def _rms(x, g):
    return x * lax.rsqrt(jnp.mean(x * x, axis=-1, keepdims=True) + EPS) * g


def _silu(x):
    return x * jax.nn.sigmoid(x)


def _dot(a, b):
    return jnp.dot(a, b, preferred_element_type=f32)


def _dot_nt(a, b):
    return lax.dot_general(a, b, (((1,), (1,)), ((), ())), preferred_element_type=f32)


IN_WIDTHS = (D_SSD, CONV_DIM, LANES, Q_DIM, K_DIM, V_DIM)


def _inproj_common(x_ref, g_ref, wz, wxbc, wdt, wq, z_o, xbc_o, dt_o):
    xb = _rms(x_ref[...], g_ref[...]).astype(bf16)
    z_o[...] = _dot(xb, wz[...])
    xbc_o[...] = _dot(xb, wxbc[...])
    dt_o[...] = _dot(xb, wdt[...])
    return xb, _dot(xb, wq[...]) * (ATTN_HEAD_DIM ** -0.5)


def _inproj_sample_body(x_ref, g_ref, wz, wxbc, wdt, wq, wk, wv, z_o, xbc_o, dt_o, q_o, k_o, v_o):
    xb, q = _inproj_common(x_ref, g_ref, wz, wxbc, wdt, wq, z_o, xbc_o, dt_o)
    q_o[...] = q.astype(bf16)
    k_o[...] = _dot(xb, wk[...])
    v_o[...] = _dot(xb, wv[...])


def _inproj_prompt_body(x_ref, g_ref, wz, wxbc, wdt, wq, wk, wv, *rest, tk, n_prev):
    z_o, xbc_o, dt_o, qT_o, kb_o, vT_o, kst_o, vst_o = rest[n_prev:]
    xb, q = _inproj_common(x_ref, g_ref, wz, wxbc, wdt, wq, z_o, xbc_o, dt_o)
    tm = q.shape[0]
    qT_o[...] = (q * LOG2E).T.astype(bf16)
    k = _dot(xb, wk[...])
    v = _dot(xb, wv[...])
    kb_o[...] = k.astype(bf16)
    vT = v.T.astype(bf16)
    ones = jnp.ones((VT_ROWS - ATTN_V_DIM, tk), bf16)
    for i in range(tm // tk):
        for h in range(ATTN_KV_HEADS):
            vT_o[i, h, 0:ATTN_V_DIM, :] = vT[h * ATTN_V_DIM:(h + 1) * ATTN_V_DIM, i * tk:(i + 1) * tk]
            vT_o[i, h, ATTN_V_DIM:VT_ROWS, :] = ones
    for h in range(ATTN_KV_HEADS):
        hs = slice(h * LANES, (h + 1) * LANES)
        kst_o[pl.ds(h, tm, stride=ATTN_KV_HEADS), :] = k[:, hs]
        vst_o[pl.ds(h, tm, stride=ATTN_KV_HEADS), :] = v[:, hs]


def _in_proj_sample(x, g, w, tm):
    m = x.shape[0]
    tm = _row_tile(m, tm)
    row = lambda n: pl.BlockSpec((tm, n), lambda i: (i, 0))
    out_dt = (f32, f32, f32, bf16, f32, f32)
    return pl.pallas_call(
        _inproj_sample_body,
        grid=(m // tm,),
        in_specs=[row(D_MODEL), _const_spec((1, D_MODEL))] + [_const_spec((D_MODEL, n)) for n in IN_WIDTHS],
        out_specs=[row(n) for n in IN_WIDTHS],
        out_shape=[jax.ShapeDtypeStruct((m, n), d) for n, d in zip(IN_WIDTHS, out_dt)],
        compiler_params=_cparams("parallel"),
        name="in_proj_sample",
    )(x, g, *w)


def _in_proj_prompt(x, g, w, prev_stacks, *, layer, depth, tm, tk):
    m = x.shape[0]
    tm = _row_tile(m, tm)
    assert tm % tk == 0
    row = lambda n: pl.BlockSpec((tm, n), lambda i: (i, 0))
    n_prev = len(prev_stacks)
    stack_spec = pl.BlockSpec((None, tm * ATTN_KV_HEADS, LANES), lambda i: (layer, i, 0))
    stack_shape = jax.ShapeDtypeStruct((depth, m * ATTN_KV_HEADS, LANES), f32)
    n_in = 2 + len(IN_WIDTHS)
    return pl.pallas_call(
        functools.partial(_inproj_prompt_body, tk=tk, n_prev=n_prev),
        grid=(m // tm,),
        in_specs=[row(D_MODEL), _const_spec((1, D_MODEL))] + [_const_spec((D_MODEL, n)) for n in IN_WIDTHS]
        + [pl.BlockSpec(memory_space=pl.ANY)] * n_prev,
        out_specs=[row(D_SSD), row(CONV_DIM), row(LANES),
                   pl.BlockSpec((Q_DIM, tm), lambda i: (0, i)), row(K_DIM),
                   pl.BlockSpec((tm // tk, ATTN_KV_HEADS, VT_ROWS, tk), lambda i: (i, 0, 0, 0)),
                   stack_spec, stack_spec],
        out_shape=[jax.ShapeDtypeStruct((m, D_SSD), f32), jax.ShapeDtypeStruct((m, CONV_DIM), f32),
                   jax.ShapeDtypeStruct((m, LANES), f32), jax.ShapeDtypeStruct((Q_DIM, m), bf16),
                   jax.ShapeDtypeStruct((m, K_DIM), bf16),
                   jax.ShapeDtypeStruct((m // tk, ATTN_KV_HEADS, VT_ROWS, tk), bf16),
                   stack_shape, stack_shape],
        input_output_aliases={n_in + j: 6 + j for j in range(n_prev)},
        compiler_params=_cparams("parallel"),
        name="in_proj_prompt",
    )(x, g, *w, *prev_stacks)


def _outmlp_body(y_ref, o_ref, x_ref, woy, woo, g_ref, wup, wdn, gf_ref, out_ref, *, final, ff_chunk):
    x1 = x_ref[...] + _dot(y_ref[...], woy[...]) + _dot(o_ref[...], woo[...])
    h = _rms(x1, g_ref[...]).astype(bf16)
    mlp = None
    for j in range(D_FF // ff_chunk):
        u = _dot(h, wup[:, j * ff_chunk:(j + 1) * ff_chunk])
        u = jnp.square(jnp.maximum(u, 0.0)).astype(bf16)
        d = _dot(u, wdn[j * ff_chunk:(j + 1) * ff_chunk, :])
        mlp = d if mlp is None else mlp + d
    x2 = x1 + mlp
    out_ref[...] = _rms(x2, gf_ref[...]) if final else x2


def _out_mlp(y, o, x, woy, woo, g, wup, wdn, gf, tm, final):
    m = x.shape[0]
    tm = _row_tile(m, tm)
    row = lambda n: pl.BlockSpec((tm, n), lambda i: (i, 0))
    return pl.pallas_call(
        functools.partial(_outmlp_body, final=final, ff_chunk=1024),
        grid=(m // tm,),
        in_specs=[row(D_SSD), row(D_ATTN), row(D_MODEL),
                  _const_spec((D_SSD, D_MODEL)), _const_spec((D_ATTN, D_MODEL)), _const_spec((1, D_MODEL)),
                  _const_spec((D_MODEL, D_FF)), _const_spec((D_FF, D_MODEL)), _const_spec((1, D_MODEL))],
        out_specs=row(D_MODEL),
        out_shape=jax.ShapeDtypeStruct((m, D_MODEL), f32),
        compiler_params=_cparams("parallel"),
        name="out_mlp",
    )(y, o, x, woy, woo, g, wup, wdn, gf)


N_PIECES = 3


def _split3(x):
    p1 = x.astype(bf16)
    r1 = x - p1.astype(f32)
    p2 = r1.astype(bf16)
    p3 = (r1 - p2.astype(f32)).astype(bf16)
    return jnp.concatenate([p1, p2, p3], axis=1)


def _expand(v, emat3):
    return _dot(_split3(v), emat3)


def _eye_bf16(reps):
    ri = lax.broadcasted_iota(jnp.int32, (LANES, reps * LANES), 0)
    ci = jnp.bitwise_and(lax.broadcasted_iota(jnp.int32, (LANES, reps * LANES), 1), LANES - 1)
    return jnp.where(ri == ci, 1.0, 0.0).astype(bf16)


def _transpose_f32(x, eye3):
    return _dot_nt(eye3, _split3(x))


def _conv_taps(cbuf, w, bias, L):
    conv = bias
    for i in range(SSD_CONV):
        off = SUBLANES - N_PRE + i
        conv = conv + w[i:i + 1, :] * cbuf[off:off + L, :]
    return conv


def _intra_chunk(cm, bm, xdt, a_cs, a_cs_t, mask):
    L = cm.shape[0]
    lane = lax.broadcasted_iota(jnp.int32, (L, LANES), 1)
    halves = (lane < SSD_HEAD_DIM, lane >= SSD_HEAD_DIM)
    out = []
    for g in range(SSD_GROUPS):
        cg = cm[:, g * SSD_STATE:(g + 1) * SSD_STATE].astype(bf16)
        bg = bm[:, g * SSD_STATE:(g + 1) * SSD_STATE].astype(bf16)
        cbm = _dot_nt(cg, bg)
        blocks = []
        for jp in range(HEADS_PER_GROUP // 2):
            pair = g * (HEADS_PER_GROUP // 2) + jp
            xp = xdt[:, pair * LANES:(pair + 1) * LANES]
            acc = None
            for k in range(2):
                h = 2 * pair + k
                seg = a_cs[:, h:h + 1] - a_cs_t[h:h + 1, :]
                dec = jnp.where(mask, jnp.exp2(seg), 0.0)
                mh = (cbm * dec).astype(bf16)
                part = _dot(mh, jnp.where(halves[k], xp, 0.0).astype(bf16))
                acc = part if acc is None else acc + part
            blocks.append(acc)
        out.append(blocks)
    return out


def _ssd_prompt_body(z_ref, xbc_ref, dt_ref, cw_ref, cb_ref, dtb_ref, alog_ref, dsk_ref, g_ref, ew_ref,
                     y_ref, hout_ref, prev_rows, hT, *, L, n_chunks):
    c = pl.program_id(1)

    @pl.when(c == 0)
    def _():
        prev_rows[...] = jnp.zeros_like(prev_rows)
        hT[...] = jnp.zeros_like(hT)

    xraw = xbc_ref[...]
    w = cw_ref[...]
    conv = cb_ref[...] + w[N_PRE:N_PRE + 1, :] * xraw
    row8 = lax.broadcasted_iota(jnp.int32, (SUBLANES, CONV_DIM), 0)
    for i in range(1, SSD_CONV):
        xr = pltpu.roll(xraw, i, 0)
        head = jnp.where(row8 < i, pltpu.roll(prev_rows[...], i, 0), xr[0:SUBLANES, :])
        conv = conv + w[N_PRE - i:N_PRE - i + 1, :] * jnp.concatenate([head, xr[SUBLANES:, :]], axis=0)
    act = _silu(conv)
    prev_rows[...] = xraw[L - SUBLANES:L, :]
    xs = act[:, :D_SSD]
    bm = act[:, D_SSD:D_SSD + SSD_GROUPS * SSD_STATE]
    cm = act[:, D_SSD + SSD_GROUPS * SSD_STATE:]

    rows = lax.broadcasted_iota(jnp.int32, (L, LANES), 0)
    dt = jax.nn.softplus(dt_ref[...] + dtb_ref[...])
    a = dt * (-jnp.exp(alog_ref[...]))
    a_cs = a
    s = 1
    while s < L:
        a_cs = a_cs + jnp.where(rows >= s, pltpu.roll(a_cs, s, 0), 0.0)
        s *= 2
    ea = jnp.exp(a_cs)
    te = jnp.exp(a_cs[L - 1:L, :] - a_cs)

    head_lane = lax.broadcasted_iota(jnp.int32, (L, LANES), 1) < SSD_HEADS
    packed = (jnp.where(head_lane, dt, 0.0) + pltpu.roll(jnp.where(head_lane, ea, 0.0), SSD_HEADS, 1)
              + pltpu.roll(jnp.where(head_lane, te, 0.0), 2 * SSD_HEADS, 1))
    wide = _dot(_split3(packed), ew_ref[...])
    ea_e = wide[:, D_SSD:2 * D_SSD]
    xdt = xs * wide[:, 0:D_SSD]
    wst = (xdt * wide[:, 2 * D_SSD:3 * D_SSD]).astype(bf16)

    eye = _eye_bf16(1)
    a_l2 = a_cs * LOG2E
    a_cs_t = _transpose_f32(a_l2, _eye_bf16(N_PIECES))
    causal = lax.broadcasted_iota(jnp.int32, (L, L), 0) >= lax.broadcasted_iota(jnp.int32, (L, L), 1)
    intra = _intra_chunk(cm, bm, xdt, a_l2, a_cs_t, causal)

    y_parts = []
    for g in range(SSD_GROUPS):
        gs = slice(g * GROUP_W, (g + 1) * GROUP_W)
        cg = cm[:, g * SSD_STATE:(g + 1) * SSD_STATE].astype(bf16)
        bg = bm[:, g * SSD_STATE:(g + 1) * SSD_STATE].astype(bf16)
        h_g = hT[:, gs]
        y_inter = _dot(cg, h_g.astype(bf16)) * ea_e[:, gs]
        bg_t = _dot_nt(eye, bg).astype(bf16)
        hT[:, gs] = h_g * ea_e[L - 1:L, gs] + _dot(bg_t, wst[:, gs])
        y_parts.append(jnp.concatenate(intra[g], axis=1) + y_inter)
    y = jnp.concatenate(y_parts, axis=1) + xs * dsk_ref[...]
    y = y * _silu(z_ref[...])
    y_ref[...] = _rms(y, g_ref[...]).astype(y_ref.dtype)

    @pl.when(c == n_chunks - 1)
    def _():
        hout_ref[...] = hT[...].T


def _ssd_weight_specs():
    return [_const_spec((SSD_CONV, CONV_DIM)), _const_spec((1, CONV_DIM)), _const_spec((1, LANES)),
            _const_spec((1, LANES)), _const_spec((1, D_SSD)), _const_spec((1, D_SSD)),
            _const_spec((N_PIECES * LANES, D_SSD))]


def _ssd_prompt(z, xbc, dt, ssd_w, emat_wide, *, batch, L, n_chunks):
    hp = SSD_HEADS * SSD_HEAD_DIM
    rowspec = lambda n: pl.BlockSpec((L, n), lambda b, c: (b * n_chunks + c, 0))
    return pl.pallas_call(
        functools.partial(_ssd_prompt_body, L=L, n_chunks=n_chunks),
        grid=(batch, n_chunks),
        in_specs=[rowspec(D_SSD), rowspec(CONV_DIM), rowspec(LANES)] + _ssd_weight_specs()[:-1]
        + [_const_spec((N_PIECES * LANES, 3 * D_SSD))],
        out_specs=[rowspec(D_SSD), pl.BlockSpec((None, hp, SSD_STATE), lambda b, c: (b, 0, 0))],
        out_shape=[jax.ShapeDtypeStruct((batch * n_chunks * L, D_SSD), bf16),
                   jax.ShapeDtypeStruct((batch, hp, SSD_STATE), f32)],
        scratch_shapes=[pltpu.VMEM((SUBLANES, CONV_DIM), f32), pltpu.VMEM((SSD_STATE, hp), f32)],
        compiler_params=_cparams("parallel", "arbitrary"),
        name="ssd_prompt",
    )(z, xbc, dt, *ssd_w[:-1], emat_wide)


def _ssd_sample_body(z_ref, xbc_ref, dt_ref, pre_ref, h0_ref, cw_ref, cb_ref, dtb_ref, alog_ref, dsk_ref, g_ref,
                     e_ref, *rest, n_seq, ts):
    y_ref, hout_ref, cbuf, yint = rest[-4:]
    R = SAMPLE_ROWS
    L = n_seq * R
    first = R - ts

    cbuf[0:SUBLANES, :] = jnp.zeros((SUBLANES, CONV_DIM), f32)
    cbuf[SUBLANES:SUBLANES + L, :] = xbc_ref[...]
    for s in range(n_seq):
        r0 = SUBLANES + s * R + first - N_PRE
        cbuf[r0:r0 + N_PRE, :] = pre_ref[s]
    act = _silu(_conv_taps(cbuf, cw_ref[...], cb_ref[...], L))
    xs = act[:, :D_SSD]
    bm = act[:, D_SSD:D_SSD + SSD_GROUPS * SSD_STATE]
    cm = act[:, D_SSD + SSD_GROUPS * SSD_STATE:]

    slot = jnp.bitwise_and(lax.broadcasted_iota(jnp.int32, (L, LANES), 0), R - 1)
    dt = jnp.where(slot >= first, jax.nn.softplus(dt_ref[...] + dtb_ref[...]), 0.0)
    a = dt * (-jnp.exp(alog_ref[...]))
    a_cs = a
    a_sf = a
    s = 1
    while s < R:
        a_cs = a_cs + jnp.where(slot >= s, pltpu.roll(a_cs, s, 0), 0.0)
        a_sf = a_sf + jnp.where(slot < R - s, pltpu.roll(a_sf, L - s, 0), 0.0)
        s *= 2
    ea = jnp.exp(a_cs)
    te = jnp.exp(a_sf - a)
    cd = jnp.exp(a_cs + a_sf - a)

    emat = e_ref[...]
    ea_e = _expand(ea, emat)
    xdt = xs * _expand(dt, emat)
    w_t = (xdt * _expand(te, emat)).T
    cd_t = _expand(cd, emat).T

    a_l2 = a_cs * LOG2E
    a_cs_t = _transpose_f32(a_l2, _eye_bf16(N_PIECES))
    li = lax.broadcasted_iota(jnp.int32, (L, L), 0)
    si = lax.broadcasted_iota(jnp.int32, (L, L), 1)
    mask = jnp.logical_and(li >= si, jnp.bitwise_and(li, -R) == jnp.bitwise_and(si, -R))
    intra = _intra_chunk(cm, bm, xdt, a_l2, a_cs_t, mask)

    col = lax.broadcasted_iota(jnp.int32, (GROUP_W, L), 1)
    cmb = cm.astype(bf16)
    bmb = bm.astype(bf16)
    for s in range(n_seq):
        rs = slice(s * R, (s + 1) * R)
        mine = jnp.logical_and(col >= s * R, col < (s + 1) * R)
        for g in range(SSD_GROUPS):
            gs = slice(g * GROUP_W, (g + 1) * GROUP_W)
            ns = slice(g * SSD_STATE, (g + 1) * SSD_STATE)
            h0 = h0_ref[s, gs, :]
            yint[rs, gs] = _dot_nt(cmb[rs, ns], h0.astype(bf16))
            upd = _dot(jnp.where(mine, w_t[gs, :], 0.0).astype(bf16), bmb[:, ns])
            hout_ref[s, gs, :] = h0 * cd_t[gs, s * R:s * R + 1] + upd

    y = jnp.concatenate(intra[0] + intra[1], axis=1) + yint[...] * ea_e + xs * dsk_ref[...]
    y = y * _silu(z_ref[...])
    y_ref[...] = _rms(y, g_ref[...]).astype(y_ref.dtype)


def _ssd_sample(z, xbc, dt, pre, h0, ssd_w, prev_stack, *, layer, depth, batch, ts, n_seq):
    hp = SSD_HEADS * SSD_HEAD_DIM
    L = n_seq * SAMPLE_ROWS
    assert batch % n_seq == 0
    rowspec = lambda n: pl.BlockSpec((L, n), lambda i: (i, 0))
    seqspec = lambda a, b: pl.BlockSpec((None, n_seq, a, b), lambda i: (layer, i, 0, 0))
    n_prev = len(prev_stack)
    n_in = 5 + len(_ssd_weight_specs())
    return pl.pallas_call(
        functools.partial(_ssd_sample_body, n_seq=n_seq, ts=ts),
        grid=(batch // n_seq,),
        in_specs=[rowspec(D_SSD), rowspec(CONV_DIM), rowspec(LANES), seqspec(N_PRE, CONV_DIM),
                  seqspec(hp, SSD_STATE)] + _ssd_weight_specs() + [pl.BlockSpec(memory_space=pl.ANY)] * n_prev,
        out_specs=[rowspec(D_SSD), seqspec(hp, SSD_STATE)],
        out_shape=[jax.ShapeDtypeStruct((batch * SAMPLE_ROWS, D_SSD), bf16),
                   jax.ShapeDtypeStruct((depth, batch, hp, SSD_STATE), f32)],
        input_output_aliases={n_in + j: 1 + j for j in range(n_prev)},
        scratch_shapes=[pltpu.VMEM((SUBLANES + L, CONV_DIM), f32), pltpu.VMEM((L, D_SSD), f32)],
        compiler_params=_cparams("parallel"),
        name="ssd_sample",
    )(z, xbc, dt, pre, h0, *ssd_w, *prev_stack)


def _lambda(lamp_ref, lam_init):
    lp = lamp_ref[...]
    t1 = jnp.sum(lp[0:1, :] * lp[1:2, :], axis=-1, keepdims=True)
    t2 = jnp.sum(lp[2:3, :] * lp[3:4, :], axis=-1, keepdims=True)
    return jnp.exp(t1) - jnp.exp(t2) + lam_init


def _mask_maps(qr, lane):
    zero = jnp.zeros_like(qr)
    return jnp.where(lane < ATTN_HEAD_DIM, qr, zero), jnp.where(lane >= ATTN_HEAD_DIM, qr, zero)


def _pattn_body(slopes_ref, qT_ref, k_ref, vT_ref, lamp_ref, g_ref, o_ref,
                q4T, sb_ref, s0, s1, p0, p1, al0, al1, mx0, mx1, m_ref, accT, *, tq, tk, lam_init, unroll):
    kv = pl.program_id(1)
    qi = pl.program_id(2)
    n_grp = 2 * ATTN_REP
    hd = ATTN_HEAD_DIM
    n_diag = tq // tk
    n_full = qi * n_diag

    zero = jnp.zeros((hd, tq), bf16)
    for r in range(ATTN_REP):
        c1 = slice(r * tq, (r + 1) * tq)
        c2 = slice((ATTN_REP + r) * tq, (ATTN_REP + r + 1) * tq)
        q4T[0:hd, c1] = qT_ref[r * LANES:r * LANES + hd, :]
        q4T[hd:2 * hd, c1] = zero
        q4T[0:hd, c2] = zero
        q4T[hd:2 * hd, c2] = qT_ref[r * LANES + hd:(r + 1) * LANES, :]
    m_ref[...] = jnp.full_like(m_ref, NEG)
    accT[...] = jnp.zeros_like(accT)
    s_buf, p_buf, al_buf, mx_buf = (s0, s1), (p0, p1), (al0, al1), (mx0, mx1)
    al1[...] = jnp.ones_like(al1)
    p1[...] = jnp.zeros_like(p1)

    rel = (lax.broadcasted_iota(jnp.int32, (tk, tq), 1) - lax.broadcasted_iota(jnp.int32, (tk, tq), 0)).astype(f32)
    slope = [slopes_ref[kv * ATTN_REP + r] * LOG2E for r in range(ATTN_REP)]

    @pl.when(qi == 0)
    def _():
        for r in range(ATTN_REP):
            sb_ref[r] = slope[r] * rel

    def scores(j, slot):
        start = pl.multiple_of(j * tk, tk)
        kblk = k_ref[pl.ds(start, tk), :]
        for g in range(n_grp):
            cs = slice(g * tq, (g + 1) * tq)
            s = _dot(kblk, q4T[:, cs]) - sb_ref[g % ATTN_REP]
            s_buf[slot][:, cs] = s
            mx_buf[slot][:, cs] = jnp.max(s, axis=0, keepdims=True)

    def softmax(j, slot, diag):
        blk_dist = (qi * tq - j * tk).astype(f32)
        for g in range(n_grp):
            cs = slice(g * tq, (g + 1) * tq)
            s = s_buf[slot][:, cs]
            if diag is None:
                mx = mx_buf[slot][:, cs]
            else:
                s = jnp.where(rel >= float(diag * tk), s, NEG)
                mx = jnp.max(s, axis=0, keepdims=True)
            off = slope[g % ATTN_REP] * blk_dist
            m_old = m_ref[:, cs]
            m_new = jnp.maximum(m_old, mx - off)
            m_ref[:, cs] = m_new
            al_buf[slot][:, cs] = jnp.exp2(m_old - m_new)
            p_buf[slot][:, cs] = jnp.exp2(s - (m_new + off)).astype(bf16)

    def values(j, slot):
        accT[...] = accT[...] * al_buf[slot][...] + _dot(vT_ref[j], p_buf[slot][...])

    def stage(j, slot, diag=None, more=True):
        if more:
            scores(j + 1, 1 - slot)
        softmax(j, slot, diag)
        values(jnp.maximum(j - 1, 0), 1 - slot)

    scores(0, 0)

    shift = unroll.bit_length() - 1

    def group_body(i, carry):
        for u in range(unroll):
            stage(unroll * i + u, u % 2)
        return carry

    lax.fori_loop(0, jnp.right_shift(n_full, shift), group_body, 0)
    rem = jnp.bitwise_and(n_full, unroll - 1)
    base = n_full - rem
    for u in range(unroll - 1):
        @pl.when(rem > u)
        def _():
            stage(base + u, u % 2)

    def tail(parity):
        for d in range(n_diag):
            stage(n_full + d, (parity + d) % 2, diag=d, more=d + 1 < n_diag)
        values(n_full + n_diag - 1, (parity + n_diag - 1) % 2)

    if n_diag % 2 == 0:
        tail(0)
    else:
        for parity in range(2):
            @pl.when(jnp.bitwise_and(n_full, 1) == parity)
            def _():
                tail(parity)

    lam = _lambda(lamp_ref, lam_init)
    for r in range(ATTN_REP):
        c1 = slice(r * tq, (r + 1) * tq)
        c2 = slice((ATTN_REP + r) * tq, (ATTN_REP + r + 1) * tq)
        nv = ATTN_V_DIM
        oT = (accT[0:nv, c1] / accT[nv:nv + 1, c1] - lam * (accT[0:nv, c2] / accT[nv:nv + 1, c2]))
        oT = oT * lax.rsqrt(jnp.mean(oT * oT, axis=0, keepdims=True) + EPS) * g_ref[...] * (1.0 - lam_init)
        o_ref[:, r * LANES:(r + 1) * LANES] = oT.T.astype(o_ref.dtype)


def _prompt_attention(slopes, qT, kb, vT, lamp, g_col, *, batch, T, tq, tk, lam_init, unroll):
    assert T % tq == 0 and tq % tk == 0
    nq = T // tq
    cols = 2 * ATTN_REP * tq
    row_stat = pltpu.VMEM((1, cols), f32)
    return pl.pallas_call(
        functools.partial(_pattn_body, tq=tq, tk=tk, lam_init=lam_init, unroll=unroll),
        grid=(batch, ATTN_KV_HEADS, nq),
        in_specs=[pl.BlockSpec(memory_space=pltpu.SMEM),
                  pl.BlockSpec((ATTN_REP * LANES, tq), lambda b, h, i: (h, b * nq + i)),
                  pl.BlockSpec((T, LANES), lambda b, h, i: (b, h)),
                  pl.BlockSpec((T // tk, None, VT_ROWS, tk), lambda b, h, i: (b, h, 0, 0)),
                  _const_spec((4, ATTN_HEAD_DIM)), _const_spec((ATTN_V_DIM, 1))],
        out_specs=pl.BlockSpec((tq, ATTN_REP * LANES), lambda b, h, i: (b * nq + i, h)),
        out_shape=jax.ShapeDtypeStruct((batch * T, D_ATTN), bf16),
        scratch_shapes=[pltpu.VMEM((LANES, cols), bf16), pltpu.VMEM((ATTN_REP, tk, tq), f32),
                        pltpu.VMEM((tk, cols), f32), pltpu.VMEM((tk, cols), f32),
                        pltpu.VMEM((tk, cols), bf16), pltpu.VMEM((tk, cols), bf16),
                        row_stat, row_stat, row_stat, row_stat, row_stat,
                        pltpu.VMEM((VT_ROWS, cols), f32)],
        compiler_params=_cparams("parallel", "parallel", "arbitrary"),
        name="prompt_attention",
    )(slopes, qT, kb, vT, lamp, g_col)


def _sattn_body(pt_ref, slopes_ref, q_ref, kn_ref, vn_ref, *rest, n_pages, page, ts, lam_init):
    k_pages = rest[:n_pages]
    v_pages = rest[n_pages:2 * n_pages]
    lamp_ref, g_ref, o_ref = rest[2 * n_pages:]
    del pt_ref
    past = n_pages * page
    R = SAMPLE_ROWS
    first = R - ts
    width = past + page

    row = lax.broadcasted_iota(jnp.int32, (R, width), 0)
    col = lax.broadcasted_iota(jnp.int32, (R, width), 1)
    is_new = col >= past
    kpos = col - jnp.where(is_new, first, 0)
    dist = (past + row - first - kpos).astype(f32)
    ok = jnp.logical_and(dist >= 0.0, jnp.logical_or(jnp.logical_not(is_new), col >= past + first))
    lane = lax.broadcasted_iota(jnp.int32, (R, LANES), 1)
    lam = _lambda(lamp_ref, lam_init)
    q = q_ref[...].astype(f32)
    zpad = jnp.zeros((page - R, LANES), bf16)

    heads = range(ATTN_KV_HEADS)
    scores = []
    for h in heads:
        groups = []
        for r in range(ATTN_REP):
            c0 = (h * ATTN_REP + r) * LANES
            groups.append(_mask_maps(q[:, c0:c0 + LANES], lane))
        q32 = jnp.concatenate([groups[0][0], groups[1][0], groups[0][1], groups[1][1]], axis=0).astype(bf16)
        hs = slice(h * LANES, (h + 1) * LANES)
        head_rows = pl.ds(h, page, stride=ATTN_KV_HEADS)
        kcat = jnp.concatenate([kp[head_rows, :].astype(bf16) for kp in k_pages]
                               + [kn_ref[:, hs].astype(bf16), zpad], axis=0)
        scores.append(_dot_nt(q32, kcat))
    probs = []
    sums = []
    for h in heads:
        ps = []
        ls = []
        for g in range(2 * ATTN_REP):
            slope = slopes_ref[h * ATTN_REP + g % ATTN_REP]
            sg = jnp.where(ok, scores[h][g * R:(g + 1) * R, :] - slope * dist, NEG)
            p = jnp.exp(sg - jnp.max(sg, axis=-1, keepdims=True))
            ls.append(jnp.sum(p, axis=-1, keepdims=True))
            ps.append(p.astype(bf16))
        probs.append(jnp.concatenate(ps, axis=0))
        sums.append(ls)
    for h in heads:
        hs = slice(h * LANES, (h + 1) * LANES)
        head_rows = pl.ds(h, page, stride=ATTN_KV_HEADS)
        vcat = jnp.concatenate([vp[head_rows, :].astype(bf16) for vp in v_pages]
                               + [vn_ref[:, hs].astype(bf16), zpad], axis=0)
        acc = _dot(probs[h], vcat)
        ls = sums[h]
        for r in range(ATTN_REP):
            g1, g2 = r, ATTN_REP + r
            o = acc[g1 * R:(g1 + 1) * R, :] / ls[g1] - lam * (acc[g2 * R:(g2 + 1) * R, :] / ls[g2])
            o = _rms(o, g_ref[...]) * (1.0 - lam_init)
            c0 = (h * ATTN_REP + r) * LANES
            o_ref[:, c0:c0 + LANES] = o.astype(o_ref.dtype)


def _sample_attention(page_table, slopes, q, kn, vn, cache_k, cache_v, lamp, g, *, layer, ts, lam_init):
    bs, n_pages = page_table.shape
    page = cache_k.shape[2] // ATTN_KV_HEADS
    R = SAMPLE_ROWS

    def page_spec(p):
        return pl.BlockSpec((None, None, page * ATTN_KV_HEADS, LANES), lambda b, pt: (layer, pt[b, p], 0, 0))

    seq = lambda n: pl.BlockSpec((R, n), lambda b, pt: (b, 0))
    const = lambda shape: pl.BlockSpec(shape, lambda b, pt: (0,) * len(shape), pipeline_mode=pl.Buffered(1))
    grid_spec = pltpu.PrefetchScalarGridSpec(
        num_scalar_prefetch=1,
        grid=(bs,),
        in_specs=[pl.BlockSpec(memory_space=pltpu.SMEM), seq(Q_DIM), seq(K_DIM), seq(V_DIM)]
        + [page_spec(p) for p in range(n_pages)] * 2
        + [const((4, ATTN_HEAD_DIM)), const((1, ATTN_V_DIM))],
        out_specs=seq(D_ATTN),
    )
    return pl.pallas_call(
        functools.partial(_sattn_body, n_pages=n_pages, page=page, ts=ts, lam_init=lam_init),
        grid_spec=grid_spec,
        out_shape=jax.ShapeDtypeStruct((bs * R, D_ATTN), bf16),
        compiler_params=_cparams("parallel"),
        name="sample_attention",
    )(page_table, slopes, q, kn, vn, *([cache_k] * n_pages), *([cache_v] * n_pages), lamp, g)


def _alibi_slopes():
    return jnp.asarray(np.array([2.0 ** (-8.0 * (i + 1) / ATTN_HEADS) for i in range(ATTN_HEADS)], np.float32))


def _expand_matrix():
    e = np.zeros((LANES, D_SSD), np.float32)
    for h in range(SSD_HEADS):
        e[h, h * SSD_HEAD_DIM:(h + 1) * SSD_HEAD_DIM] = 1.0
    return jnp.asarray(np.tile(e, (N_PIECES, 1)), bf16)


def _expand_matrix_wide():
    e = np.zeros((LANES, 3 * D_SSD), np.float32)
    for k in range(3):
        for h in range(SSD_HEADS):
            c0 = k * D_SSD + h * SSD_HEAD_DIM
            e[k * SSD_HEADS + h, c0:c0 + SSD_HEAD_DIM] = 1.0
    return jnp.asarray(np.tile(e, (N_PIECES, 1)), bf16)


def _pad_lanes(v):
    return jnp.pad(v, (0, LANES - v.shape[0]))[None, :]


def kernel(x_prompt, x_sample, cache_k, cache_v, state_ssm, state_conv, page_table, norm_mix, w_in, conv_w, conv_b, dt_bias, A_log, D_skip, ssd_norm, lambda_q1, lambda_k1, lambda_q2, lambda_k2, attn_subln, w_out, norm_mlp, w_up, w_down, norm_final):
    depth = w_in.shape[0]
    bp, T, _ = x_prompt.shape
    bs, ts, _ = x_sample.shape
    R = SAMPLE_ROWS
    assert T % SSD_CHUNK == 0 and N_PRE <= ts <= R - N_PRE and ts % SSD_CHUNK != 0
    mp = bp * T
    hp = SSD_HEADS * SSD_HEAD_DIM

    slopes = _alibi_slopes()
    emat = _expand_matrix()
    emat_wide = _expand_matrix_wide()
    gf = norm_final[None, :]

    xp = x_prompt.reshape(mp, D_MODEL)
    xs = jnp.pad(x_sample, ((0, 0), (R - ts, 0), (0, 0))).reshape(bs * R, D_MODEL)
    state4 = state_ssm.reshape(depth, bs, hp, SSD_STATE)

    o0 = D_SSD
    o1 = o0 + CONV_DIM
    o2 = o1 + SSD_HEADS
    o3 = o2 + Q_DIM
    o4 = o3 + K_DIM

    n_pool = cache_k.shape[1]
    page_rows = cache_k.shape[2] * ATTN_KV_HEADS
    cache_k2 = cache_k.reshape(depth, n_pool, page_rows, LANES)
    cache_v2 = cache_v.reshape(depth, n_pool, page_rows, LANES)

    outs = {k: [] for k in ("hp", "cp", "ks", "vs", "cs")}
    kv_stacks = ()
    hs_stack = ()
    for l in range(depth):
        lam_init = 0.8 - 0.6 * math.exp(-0.3 * l)
        wl = w_in[l]
        w_parts = (wl[:, :o0].astype(bf16), wl[:, o0:o1].astype(bf16),
                   jnp.pad(wl[:, o1:o2], ((0, 0), (0, LANES - SSD_HEADS))).astype(bf16),
                   wl[:, o2:o3].astype(bf16), wl[:, o3:o4].astype(bf16), wl[:, o4:].astype(bf16))
        g_mix = norm_mix[l][None, :]
        ssd_w = (conv_w[l], conv_b[l][None, :], _pad_lanes(dt_bias[l]), _pad_lanes(A_log[l]),
                 jnp.repeat(D_skip[l], SSD_HEAD_DIM)[None, :], ssd_norm[l][None, :], emat)
        lamp = jnp.stack([lambda_q1[l], lambda_k1[l], lambda_q2[l], lambda_k2[l]])
        g_sub = attn_subln[l][None, :]
        woy = w_out[l, :D_SSD].astype(bf16)
        woo = w_out[l, D_SSD:].astype(bf16)
        mlp_w = (norm_mlp[l][None, :], w_up[l].astype(bf16), w_down[l].astype(bf16), gf)
        final = l == depth - 1

        z, xbc, dt, qT, kb, vT, k_st, v_st = _in_proj_prompt(xp, g_mix, w_parts, kv_stacks, layer=l, depth=depth,
                                                             tm=512, tk=PROMPT_TK)
        kv_stacks = (k_st, v_st)
        y, h_new = _ssd_prompt(z, xbc, dt, ssd_w, emat_wide, batch=bp, L=SSD_CHUNK, n_chunks=T // SSD_CHUNK)
        o = _prompt_attention(slopes, qT, kb, vT, lamp, g_sub.reshape(ATTN_V_DIM, 1), batch=bp, T=T,
                              tq=min(PROMPT_TQ, T), tk=PROMPT_TK, lam_init=lam_init, unroll=2)
        xp = _out_mlp(y, o, xp, woy, woo, *mlp_w, 512, final)
        outs["hp"].append(h_new.reshape(bp, SSD_HEADS, SSD_HEAD_DIM, SSD_STATE))
        outs["cp"].append(xbc.reshape(bp, T, CONV_DIM)[:, T - N_PRE:, :])

        z, xbc, dt, q, k, v = _in_proj_sample(xs, g_mix, w_parts, 512)
        y, hs_new = _ssd_sample(z, xbc, dt, state_conv, state4, ssd_w, hs_stack, layer=l, depth=depth, batch=bs,
                                ts=ts, n_seq=16)
        hs_stack = (hs_new,)
        o = _sample_attention(page_table, slopes, q, k, v, cache_k2, cache_v2, lamp, g_sub,
                              layer=l, ts=ts, lam_init=lam_init)
        xs = _out_mlp(y, o, xs, woy, woo, *mlp_w, 512, final)
        outs["ks"].append(k.reshape(bs, R, ATTN_KV_HEADS, 2 * ATTN_HEAD_DIM)[:, R - ts:])
        outs["vs"].append(v.reshape(bs, R, ATTN_KV_HEADS, ATTN_V_DIM)[:, R - ts:])
        outs["cs"].append(xbc.reshape(bs, R, CONV_DIM)[:, R - N_PRE:])

    y_prompt = xp.reshape(bp, T, D_MODEL)
    y_sample = xs.reshape(bs, R, D_MODEL)[:, R - ts:]
    new_k_prompt = kv_stacks[0].reshape(depth, bp, T, ATTN_KV_HEADS, 2 * ATTN_HEAD_DIM)
    new_v_prompt = kv_stacks[1].reshape(depth, bp, T, ATTN_KV_HEADS, ATTN_V_DIM)
    new_ssm_sample = hs_stack[0].reshape(depth, bs, SSD_HEADS, SSD_HEAD_DIM, SSD_STATE)
    st = lambda key: jnp.stack(outs[key])
    return (y_prompt, y_sample, new_k_prompt, new_v_prompt, st("hp"), st("cp"), st("ks"), st("vs"),
            new_ssm_sample, st("cs"))
```
